```python
import jax, jax.numpy as jnp
from jax import lax
import numpy as np

D_MODEL = 2048
BATCH = 2
SEQ = 8192
DEPTH = 1

N_META = 16
CHUNK = 64
N_HEADS_A = 8
HEAD_DIM_K = 128
HEAD_DIM_V = 128
QK_DIM = N_HEADS_A * HEAD_DIM_K
V_DIM = N_HEADS_A * HEAD_DIM_V
CONV_K = 4
POOL_WINDOWS = (2, 4, 8, 16)
N_POOL_GROUPS = 4
POOL_GROUP_DIM = 256
POOL_DIM = N_POOL_GROUPS * POOL_GROUP_DIM
FFN_DIM = -(-(8 * D_MODEL) // (3 * 256)) * 256
RMS_EPS = 1e-6
L2_EPS = 1e-6
IN_SIZES = (QK_DIM, QK_DIM, V_DIM, V_DIM, N_HEADS_A, N_HEADS_A, POOL_DIM, D_MODEL, D_MODEL)
IN_DIM = QK_DIM * 2 + V_DIM * 2 + N_HEADS_A * 2 + POOL_DIM + D_MODEL * 2

kernel_name = 'hybrid_gated_deltanet_multiscale_pool_block'


def rms_norm(x, w):
    xf = x.astype(jnp.float32)
    y = xf * lax.rsqrt(jnp.mean(xf * xf, axis=-1, keepdims=True) + RMS_EPS)
    return (y * w.astype(jnp.float32)).astype(x.dtype)


def l2_norm(x):
    return x * lax.rsqrt(jnp.sum(x * x, axis=-1, keepdims=True) + L2_EPS)


def causal_dwconv(x, w):
    return lax.conv_general_dilated(
        x, w.astype(x.dtype)[:, None, :], window_strides=(1,), padding=[(CONV_K - 1, 0)],
        dimension_numbers=('NWC', 'WIO', 'NWC'), feature_group_count=x.shape[-1])


def gated_delta_rule_chunked(q, k, v, beta, g):
    B, T, H, DK = q.shape
    DV = v.shape[-1]
    N = T // CHUNK

    def to_chunks(a):
        return a.reshape(B, N, CHUNK, H, a.shape[-1]).transpose(0, 1, 3, 2, 4)

    qc = to_chunks(q * (DK ** -0.5))
    kc = to_chunks(k)
    vc = to_chunks(v)
    bc = beta.reshape(B, N, CHUNK, H).transpose(0, 1, 3, 2)
    gc = jnp.cumsum(g.reshape(B, N, CHUNK, H).transpose(0, 1, 3, 2), axis=-1)

    incl = jnp.tril(jnp.ones((CHUNK, CHUNK), dtype=bool))
    strict = jnp.tril(jnp.ones((CHUNK, CHUNK), dtype=bool), k=-1)
    diff = gc[..., :, None] - gc[..., None, :]
    decay = jnp.exp(jnp.where(incl, diff, -jnp.inf))

    kb = kc * bc[..., None]
    a_mat = jnp.where(strict, jnp.einsum('bnhid,bnhjd->bnhij', kb, kc) * decay, 0.0)
    eye = jnp.broadcast_to(jnp.eye(CHUNK, dtype=jnp.float32), a_mat.shape)
    t_mat = lax.linalg.triangular_solve(a_mat, eye, left_side=True, lower=True, unit_diagonal=True)
    u = jnp.einsum('bnhij,bnhjd->bnhid', t_mat, vc * bc[..., None])
    w = jnp.einsum('bnhij,bnhjd->bnhid', t_mat, kb * jnp.exp(gc)[..., None])
    qk = jnp.einsum('bnhid,bnhjd->bnhij', qc, kc) * decay
    q_dec = qc * jnp.exp(gc)[..., None]
    k_dec = kc * jnp.exp(gc[..., -1:] - gc)[..., None]
    g_last = jnp.exp(gc[..., -1])

    def step(state, inp):
        u_n, w_n, qk_n, qd_n, kd_n, gl_n = inp
        v_new = u_n - jnp.einsum('bhcd,bhde->bhce', w_n, state)
        o_n = jnp.einsum('bhcd,bhde->bhce', qd_n, state) + jnp.einsum('bhij,bhje->bhie', qk_n, v_new)
        state = state * gl_n[..., None, None] + jnp.einsum('bhcd,bhce->bhde', kd_n, v_new)
        return state, o_n

    xs = tuple(jnp.moveaxis(a, 1, 0) for a in (u, w, qk, q_dec, k_dec, g_last))
    s0 = jnp.zeros((B, H, DK, DV), jnp.float32)
    _, o = lax.scan(step, s0, xs)
    return o.transpose(1, 0, 3, 2, 4).reshape(B, T, H, DV)


def causal_multiscale_pool(p):
    B, L, _ = p.shape
    pg = p.reshape(B, L, N_POOL_GROUPS, POOL_GROUP_DIM)
    csum = jnp.cumsum(pg, axis=1)
    t = jnp.arange(L)
    outs = []
    for gi, win in enumerate(POOL_WINDOWS):
        cg = csum[:, :, gi]
        shifted = jnp.pad(cg[:, :L - win], ((0, 0), (win, 0), (0, 0)))
        count = jnp.minimum(t + 1, win).astype(jnp.float32)[None, :, None]
        outs.append((cg - shifted) / count - pg[:, :, gi])
    return jnp.stack(outs, axis=2)


def hybrid_mixer(hn, w_in, conv_w, a_log, dt_bias, o_norm_w, w_proj_a, pool_w, pool_scale, w_proj_b, w_out):
    B, L, _ = hn.shape
    dt = hn.dtype
    split_idx = np.cumsum(IN_SIZES)[:-1].tolist()
    proj = hn @ w_in
    q, k, v, z, b_logit, a_logit, p, gate_a, gate_b = jnp.split(proj, split_idx, axis=-1)

    qkv = jax.nn.silu(causal_dwconv(jnp.concatenate([q, k, v], axis=-1), conv_w)).astype(jnp.float32)
    q, k, v = jnp.split(qkv, [QK_DIM, 2 * QK_DIM], axis=-1)
    q = l2_norm(q.reshape(B, L, N_HEADS_A, HEAD_DIM_K))
    k = l2_norm(k.reshape(B, L, N_HEADS_A, HEAD_DIM_K))
    v = v.reshape(B, L, N_HEADS_A, HEAD_DIM_V)
    beta = jax.nn.sigmoid(b_logit.astype(jnp.float32))
    g = -jnp.exp(a_log.astype(jnp.float32)) * jax.nn.softplus(
        a_logit.astype(jnp.float32) + dt_bias.astype(jnp.float32))
    pad = (-N_META) % CHUNK
    padt = lambda a: jnp.pad(a, ((0, 0), (pad, 0)) + ((0, 0),) * (a.ndim - 2))
    o = gated_delta_rule_chunked(padt(q), padt(k), padt(v), padt(beta), padt(g))[:, pad:]
    zf = z.astype(jnp.float32).reshape(B, L, N_HEADS_A, HEAD_DIM_V)
    o = o * lax.rsqrt(jnp.mean(o * o, axis=-1, keepdims=True) + RMS_EPS) * o_norm_w.astype(jnp.float32) * jax.nn.silu(zf)
    y_a = o.reshape(B, L, V_DIM).astype(dt) @ w_proj_a

    pooled = causal_multiscale_pool(p.astype(jnp.float32))
    zb = jnp.einsum('blgc,gcd->blgd', pooled, pool_w.astype(jnp.float32)).reshape(B, L, POOL_DIM)
    zb = zb * pool_scale.astype(jnp.float32)
    y_b = zb.astype(dt) @ w_proj_b

    merged = jax.nn.sigmoid(gate_a) * y_a + jax.nn.sigmoid(gate_b) * y_b
    return merged @ w_out


def swiglu_ffn(hn, w_gate, w_up, w_down):
    return (jax.nn.silu(hn @ w_gate) * (hn @ w_up)) @ w_down


def setup_inputs(seed: int = 0) -> dict:
    key = jax.random.key(seed)
    ks = jax.random.split(key, 20)
    f32 = jnp.float32
    nrm = lambda k, shape, scale: jax.random.normal(k, shape, f32) * scale
    gain = lambda k, shape: 1.0 + 0.02 * jax.random.normal(k, shape, f32)
    dt_init = jnp.exp(jax.random.uniform(ks[5], (DEPTH, N_HEADS_A), f32, np.log(1e-3), np.log(1e-1)))
    return {
        'x': nrm(ks[0], (BATCH, SEQ, D_MODEL), 1.0),
        'meta_tokens': nrm(ks[1], (N_META, D_MODEL), 1.0),
        'norm_mix_w': gain(ks[2], (DEPTH, D_MODEL)),
        'w_in': nrm(ks[3], (DEPTH, D_MODEL, IN_DIM), D_MODEL ** -0.5),
        'conv_w': nrm(ks[4], (DEPTH, CONV_K, 2 * QK_DIM + V_DIM), CONV_K ** -0.5),
        'a_log': jnp.log(jax.random.uniform(ks[6], (DEPTH, N_HEADS_A), f32, 1.0, 16.0)),
        'dt_bias': dt_init + jnp.log(-jnp.expm1(-dt_init)),
        'o_norm_w': gain(ks[7], (DEPTH, HEAD_DIM_V)),
        'w_proj_a': nrm(ks[8], (DEPTH, V_DIM, D_MODEL), V_DIM ** -0.5),
        'pool_w': nrm(ks[9], (DEPTH, N_POOL_GROUPS, POOL_GROUP_DIM, POOL_GROUP_DIM), POOL_GROUP_DIM ** -0.5),
        'pool_scale': gain(ks[10], (DEPTH, POOL_DIM)),
        'w_proj_b': nrm(ks[11], (DEPTH, POOL_DIM, D_MODEL), POOL_DIM ** -0.5),
        'w_out': nrm(ks[12], (DEPTH, D_MODEL, D_MODEL), D_MODEL ** -0.5),
        'norm_ffn_w': gain(ks[13], (DEPTH, D_MODEL)),
        'w_ffn_gate': nrm(ks[14], (DEPTH, D_MODEL, FFN_DIM), D_MODEL ** -0.5),
        'w_ffn_up': nrm(ks[15], (DEPTH, D_MODEL, FFN_DIM), D_MODEL ** -0.5),
        'w_ffn_down': nrm(ks[16], (DEPTH, FFN_DIM, D_MODEL), FFN_DIM ** -0.5),
        'norm_final_w': gain(ks[17], (D_MODEL,)),
    }


def reference(x, meta_tokens, norm_mix_w, w_in, conv_w, a_log, dt_bias, o_norm_w, w_proj_a, pool_w,
              pool_scale, w_proj_b, w_out, norm_ffn_w, w_ffn_gate, w_ffn_up, w_ffn_down, norm_final_w):
    B = x.shape[0]
    meta = jnp.broadcast_to(meta_tokens.astype(x.dtype)[None], (B, N_META, D_MODEL))
    h = jnp.concatenate([meta, x], axis=1)
    for l in range(DEPTH):
        h = h + hybrid_mixer(rms_norm(h, norm_mix_w[l]), w_in[l], conv_w[l], a_log[l], dt_bias[l], o_norm_w[l],
                             w_proj_a[l], pool_w[l], pool_scale[l], w_proj_b[l], w_out[l])
        h = h + swiglu_ffn(rms_norm(h, norm_ffn_w[l]), w_ffn_gate[l], w_ffn_up[l], w_ffn_down[l])
    h = rms_norm(h, norm_final_w)
    return h[:, N_META:]
```

```python
import functools

import jax
import jax.numpy as jnp
from jax import lax
from jax.experimental import pallas as pl
from jax.experimental.pallas import tpu as pltpu

N_META = 16
N_HEADS = 8
HEAD_DIM = 128
CONV_K = 4
POOL_WINDOWS = (2, 4, 8, 16)
POOL_GROUP_DIM = 256
RMS_EPS = 1e-6
L2_EPS = 1e-6

DELTA_CHUNK = 128
HIST_ROWS = 8

_V7X_VMEM_LIMIT = 56 * 1024 * 1024

_f32 = jnp.float32
_bf16 = jnp.bfloat16


def _dot(a, b):
    return jnp.dot(a, b, preferred_element_type=_f32)


def _dot_nt(a, b):
    return lax.dot_general(a, b, (((1,), (1,)), ((), ())), preferred_element_type=_f32)


def _sigmoid(x):
    return 1.0 / (1.0 + jnp.exp(-x))


def _silu(x):
    return x * _sigmoid(x)


def _softplus(x):
    return jnp.maximum(x, 0.0) + jnp.log1p(jnp.exp(-jnp.abs(x)))


def _rms_norm(x, w):
    return x * lax.rsqrt(jnp.mean(x * x, axis=-1, keepdims=True) + RMS_EPS) * w


def _in_proj_kernel(x_ref, nw_ref, wba_ref, w_ref, out_ref, ba_ref, hn_ref):
    @pl.when(pl.program_id(1) == 0)
    def _():
        hn = _rms_norm(x_ref[...], nw_ref[...]).astype(_bf16)
        hn_ref[...] = hn
        ba_ref[...] = _dot(hn, wba_ref[...])

    out_ref[...] = _dot(hn_ref[...], w_ref[...])


def _in_proj(x, norm_w, w_main, w_ba, tm, tn):
    m, d = x.shape
    n = w_main.shape[1]
    return pl.pallas_call(
        _in_proj_kernel,
        grid=(m // tm, n // tn),
        in_specs=[
            pl.BlockSpec((tm, d), lambda i, j: (i, 0)),
            pl.BlockSpec((1, d), lambda i, j: (0, 0)),
            pl.BlockSpec((d, 128), lambda i, j: (0, 0)),
            pl.BlockSpec((d, tn), lambda i, j: (0, j)),
        ],
        out_specs=[
            pl.BlockSpec((tm, tn), lambda i, j: (i, j)),
            pl.BlockSpec((tm, 128), lambda i, j: (i, 0)),
        ],
        out_shape=[
            jax.ShapeDtypeStruct((m, n), _f32),
            jax.ShapeDtypeStruct((m, 128), _f32),
        ],
        scratch_shapes=[pltpu.VMEM((tm, d), _bf16)],
        compiler_params=pltpu.CompilerParams(
            dimension_semantics=("arbitrary", "arbitrary"),
            vmem_limit_bytes=_V7X_VMEM_LIMIT),
        name="in_proj",
    )(x, norm_w, w_ba, w_main)


def _shift_rows(x, hist, s):
    xs = pltpu.roll(x, s, axis=0)
    hs = pltpu.roll(hist, s, axis=0)
    row = lax.broadcasted_iota(jnp.int32, hist.shape, 0)
    head = jnp.where(row < s, hs, xs[:HIST_ROWS])
    return jnp.concatenate([head, xs[HIST_ROWS:]], axis=0)


def _conv_silu(x, hist, cw):
    acc = x * cw[CONV_K - 1:CONV_K]
    for s in range(1, CONV_K):
        acc = acc + _shift_rows(x, hist, s) * cw[CONV_K - 1 - s:CONV_K - s]
    return _silu(acc)


def _l2_norm(x):
    return x * lax.rsqrt(jnp.sum(x * x, axis=-1, keepdims=True) + L2_EPS)


def _head_gates(ba, alog_row, dtb_row, head):
    lane = lax.broadcasted_iota(jnp.int32, ba.shape, 1)
    beta_all = _sigmoid(ba)
    g_all = -jnp.exp(alog_row) * _softplus(ba + dtb_row)
    beta = jnp.sum(jnp.where(lane == head, beta_all, 0.0), axis=1, keepdims=True)
    g = jnp.sum(jnp.where(lane == head + N_HEADS, g_all, 0.0), axis=1, keepdims=True)
    return jnp.broadcast_to(beta, ba.shape), jnp.broadcast_to(g, ba.shape)


def _chunk_cumsum(g):
    pos = lax.broadcasted_iota(jnp.int32, g.shape, 0) % DELTA_CHUNK
    s = 1
    while s < DELTA_CHUNK:
        g = g + jnp.where(pos >= s, pltpu.roll(g, s, axis=0), 0.0)
        s *= 2
    return g


def _unit_lower_inverse(a):
    c = a.shape[0]
    row = lax.broadcasted_iota(jnp.int32, a.shape, 0)
    col = lax.broadcasted_iota(jnp.int32, a.shape, 1)
    p = jnp.where(row == col, 1.0, 0.0) - a
    ab = a.astype(_bf16)
    apow = _dot(ab, ab)
    n = 2
    while n < c:
        apb = apow.astype(_bf16)
        if 2 * n < c:
            both = _dot(jnp.concatenate([p.astype(_bf16), apb], axis=0), apb)
            p = p + both[:c]
            apow = both[c:]
        else:
            p = p + _dot(p.astype(_bf16), apb)
        n *= 2
    return p


def _delta_chunk(qn, kn, v, beta, gc, state):
    c = DELTA_CHUNK
    row = lax.broadcasted_iota(jnp.int32, (c, c), 0)
    col = lax.broadcasted_iota(jnp.int32, (c, c), 1)
    decay = jnp.exp(jnp.where(row >= col, gc - gc.T, -jnp.inf))
    eg = jnp.exp(gc)
    g_last = gc[c - 1:c]
    qs = qn * (HEAD_DIM ** -0.5)
    kb = kn.astype(_bf16)
    qk_kk = _dot_nt(jnp.concatenate([qs.astype(_bf16), kb], axis=0), kb)
    qk = qk_kk[:c] * decay
    a_mat = jnp.where(row > col, qk_kk[c:] * beta * decay, 0.0)
    t_mat = _unit_lower_inverse(a_mat)
    rhs = jnp.concatenate([v * beta, kn * (beta * eg)], axis=1).astype(_bf16)
    uw = _dot(t_mat.astype(_bf16), rhs)
    u, w = uw[:, :HEAD_DIM], uw[:, HEAD_DIM:]
    q_dec = qs * eg
    k_dec = kn * jnp.exp(g_last - gc)
    sb = state.astype(_bf16)
    ws_qs = _dot(jnp.concatenate([w, q_dec], axis=0).astype(_bf16), sb)
    v_new = u - ws_qs[:c]
    vb = v_new.astype(_bf16)
    o = ws_qs[c:] + _dot(qk.astype(_bf16), vb)
    state = state * jnp.exp(g_last) + _dot(k_dec.T.astype(_bf16), vb)
    return o, state


def _delta_kernel(q_ref, k_ref, v_ref, z_ref, ba_ref, mq_ref, mk_ref, mv_ref, mba_ref,
                  cwq_ref, cwk_ref, cwv_ref, alog_ref, dtb_ref, onw_ref,
                  o_ref, state_ref, hq_ref, hk_ref, hv_ref):
    head = pl.program_id(1)
    rows = q_ref.shape[0]
    c = DELTA_CHUNK

    def prep(q_raw, k_raw, v_raw, ba, hq, hk, hv):
        qn = _l2_norm(_conv_silu(q_raw, hq, cwq_ref[0]))
        kn = _l2_norm(_conv_silu(k_raw, hk, cwk_ref[0]))
        v = _conv_silu(v_raw, hv, cwv_ref[0])
        beta, g = _head_gates(ba, alog_ref[...], dtb_ref[...], head)
        return qn, kn, v, beta, g

    @pl.when(pl.program_id(2) == 0)
    def _():
        mq, mk, mv = mq_ref[...], mk_ref[...], mv_ref[...]
        zeros = jnp.zeros((HIST_ROWS, HEAD_DIM), _f32)
        qn, kn, v, beta, g = prep(mq, mk, mv, mba_ref[...], zeros, zeros, zeros)
        is_meta = lax.broadcasted_iota(jnp.int32, beta.shape, 0) >= c - N_META
        beta = jnp.where(is_meta, beta, 0.0)
        g = jnp.where(is_meta, g, 0.0)
        _, state = _delta_chunk(qn, kn, v, beta, _chunk_cumsum(g),
                                jnp.zeros((HEAD_DIM, HEAD_DIM), _f32))
        state_ref[...] = state
        hq_ref[...] = mq[c - HIST_ROWS:]
        hk_ref[...] = mk[c - HIST_ROWS:]
        hv_ref[...] = mv[c - HIST_ROWS:]

    q_raw, k_raw, v_raw = q_ref[...], k_ref[...], v_ref[...]
    qn, kn, v, beta, g = prep(q_raw, k_raw, v_raw, ba_ref[...], hq_ref[...], hk_ref[...], hv_ref[...])
    hq_ref[...] = q_raw[rows - HIST_ROWS:]
    hk_ref[...] = k_raw[rows - HIST_ROWS:]
    hv_ref[...] = v_raw[rows - HIST_ROWS:]
    gc = _chunk_cumsum(g)
    state = state_ref[...]
    for ci in range(rows // c):
        sl = slice(ci * c, (ci + 1) * c)
        o, state = _delta_chunk(qn[sl], kn[sl], v[sl], beta[sl], gc[sl], state)
        o = o * lax.rsqrt(jnp.mean(o * o, axis=-1, keepdims=True) + RMS_EPS)
        o_ref[sl, :] = (o * onw_ref[...] * _silu(z_ref[sl, :])).astype(o_ref.dtype)
    state_ref[...] = state


def _delta_rule(proj, ba, meta_proj, meta_ba, conv_w, alog_row, dtb_row, onw_row, batch, rb):
    m = proj.shape[0]
    nblk = m // batch // rb
    hd = HEAD_DIM
    h = N_HEADS

    def tok(col0):
        return pl.BlockSpec((rb, hd), lambda b, hh, cb: (b * nblk + cb, col0 + hh))

    def meta(col0):
        return pl.BlockSpec((DELTA_CHUNK, hd), lambda b, hh, cb: (0, col0 + hh))

    def cw(col0):
        return pl.BlockSpec((1, CONV_K, hd), lambda b, hh, cb: (col0 + hh, 0, 0))

    row128 = pl.BlockSpec((1, 128), lambda b, hh, cb: (0, 0))
    return pl.pallas_call(
        _delta_kernel,
        grid=(batch, h, nblk),
        in_specs=[
            tok(0), tok(h), tok(2 * h), tok(3 * h),
            pl.BlockSpec((rb, 128), lambda b, hh, cb: (b * nblk + cb, 0)),
            meta(0), meta(h), meta(2 * h),
            pl.BlockSpec((DELTA_CHUNK, 128), lambda b, hh, cb: (0, 0)),
            cw(0), cw(h), cw(2 * h),
            row128, row128, row128,
        ],
        out_specs=pl.BlockSpec((rb, hd), lambda b, hh, cb: (b * nblk + cb, hh)),
        out_shape=jax.ShapeDtypeStruct((m, h * hd), _bf16),
        scratch_shapes=[
            pltpu.VMEM((hd, hd), _f32),
            pltpu.VMEM((HIST_ROWS, hd), _f32),
            pltpu.VMEM((HIST_ROWS, hd), _f32),
            pltpu.VMEM((HIST_ROWS, hd), _f32),
        ],
        compiler_params=pltpu.CompilerParams(
            dimension_semantics=("arbitrary", "arbitrary", "arbitrary"),
            vmem_limit_bytes=_V7X_VMEM_LIMIT),
        name="delta_rule",
    )(proj, proj, proj, proj, ba, meta_proj, meta_proj, meta_proj, meta_ba,
      conv_w, conv_w, conv_w, alog_row, dtb_row, onw_row)


def _merge_kernel(blocks_per_batch, p_ref, pprev_ref, pmeta_ref, ga_ref, gb_ref, o_ref, x_ref,
                  poolw_ref, pscale_ref, wpa_ref, wpb_ref, wout_ref, out_ref):
    first = (pl.program_id(0) % blocks_per_batch) == 0
    prev = jnp.where(first, pmeta_ref[...], pprev_ref[...])
    p = p_ref[...]
    ext = jnp.concatenate([prev, p], axis=0)
    gd = POOL_GROUP_DIM
    zb = []
    for gi, win in enumerate(POOL_WINDOWS):
        acc = ext[:, gi * gd:(gi + 1) * gd]
        s = 1
        while s < win:
            acc = acc + pltpu.roll(acc, s, axis=0)
            s *= 2
        pooled = acc[N_META:] * (1.0 / win) - p[:, gi * gd:(gi + 1) * gd]
        zb.append(_dot(pooled.astype(_bf16), poolw_ref[gi]))
    zb = (jnp.concatenate(zb, axis=1) * pscale_ref[...]).astype(_bf16)
    y_b = _dot(zb, wpb_ref[...])
    y_a = _dot(o_ref[...], wpa_ref[...])
    merged = _sigmoid(ga_ref[...]) * y_a + _sigmoid(gb_ref[...]) * y_b
    out_ref[...] = x_ref[...] + _dot(merged.astype(_bf16), wout_ref[...])


def _merge(proj, meta_proj, o_gated, x, pool_w, pool_scale, w_proj_a, w_proj_b, w_out,
           batch, tm, col_p, col_ga, col_gb):
    m, d = x.shape
    pd = pool_w.shape[0] * POOL_GROUP_DIM
    vd = o_gated.shape[1]
    blocks_per_batch = m // batch // tm
    sub = tm // N_META

    def const(shape):
        return pl.BlockSpec(shape, lambda i: (0,) * len(shape), pipeline_mode=pl.Buffered(1))

    return pl.pallas_call(
        functools.partial(_merge_kernel, blocks_per_batch),
        grid=(m // tm,),
        in_specs=[
            pl.BlockSpec((tm, pd), lambda i: (i, col_p // pd)),
            pl.BlockSpec((N_META, pd), lambda i: (jnp.maximum(i * sub - 1, 0), col_p // pd)),
            pl.BlockSpec((N_META, pd), lambda i: (0, col_p // pd)),
            pl.BlockSpec((tm, d), lambda i: (i, col_ga // d)),
            pl.BlockSpec((tm, d), lambda i: (i, col_gb // d)),
            pl.BlockSpec((tm, vd), lambda i: (i, 0)),
            pl.BlockSpec((tm, d), lambda i: (i, 0)),
            const(pool_w.shape), const((1, pd)), const(w_proj_a.shape), const(w_proj_b.shape),
            const(w_out.shape),
        ],
        out_specs=pl.BlockSpec((tm, d), lambda i: (i, 0)),
        out_shape=jax.ShapeDtypeStruct((m, d), _f32),
        compiler_params=pltpu.CompilerParams(
            dimension_semantics=("arbitrary",),
            vmem_limit_bytes=_V7X_VMEM_LIMIT),
        name="merge",
    )(proj, proj, meta_proj, proj, proj, o_gated, x, pool_w, pool_scale, w_proj_a, w_proj_b, w_out)


def _ffn_kernel(h_ref, nw_ref, wg_ref, wu_ref, wd_ref, fw_ref, out_ref, hn_ref, acc_ref):
    f = pl.program_id(1)

    @pl.when(f == 0)
    def _():
        hn_ref[...] = _rms_norm(h_ref[...], nw_ref[...]).astype(_bf16)
        acc_ref[...] = jnp.zeros_like(acc_ref)

    hn = hn_ref[...]
    act = (_silu(_dot(hn, wg_ref[...])) * _dot(hn, wu_ref[...])).astype(_bf16)
    acc_ref[...] += _dot(act, wd_ref[...])

    @pl.when(f == pl.num_programs(1) - 1)
    def _():
        out_ref[...] = _rms_norm(h_ref[...] + acc_ref[...], fw_ref[...])


def _ffn(h, norm_w, w_gate, w_up, w_down, final_w, tm, tf):
    m, d = h.shape
    f = w_gate.shape[1]
    return pl.pallas_call(
        _ffn_kernel,
        grid=(m // tm, f // tf),
        in_specs=[
            pl.BlockSpec((tm, d), lambda i, j: (i, 0)),
            pl.BlockSpec((1, d), lambda i, j: (0, 0)),
            pl.BlockSpec((d, tf), lambda i, j: (0, j)),
            pl.BlockSpec((d, tf), lambda i, j: (0, j)),
            pl.BlockSpec((tf, d), lambda i, j: (j, 0)),
            pl.BlockSpec((1, d), lambda i, j: (0, 0)),
        ],
        out_specs=pl.BlockSpec((tm, d), lambda i, j: (i, 0)),
        out_shape=jax.ShapeDtypeStruct((m, d), _f32),
        scratch_shapes=[pltpu.VMEM((tm, d), _bf16), pltpu.VMEM((tm, d), _f32)],
        compiler_params=pltpu.CompilerParams(
            dimension_semantics=("arbitrary", "arbitrary"),
            vmem_limit_bytes=_V7X_VMEM_LIMIT),
        name="ffn",
    )(h, norm_w, w_gate, w_up, w_down, final_w)


def _largest_tile(n, cap):
    t = cap
    while n % t:
        t //= 2
    return t


def kernel(x, meta_tokens, norm_mix_w, w_in, conv_w, a_log, dt_bias, o_norm_w, w_proj_a, pool_w,
           pool_scale, w_proj_b, w_out, norm_ffn_w, w_ffn_gate, w_ffn_up, w_ffn_down, norm_final_w):
    assert w_in.shape[0] == 1, "single layer block"
    batch, seq, d = x.shape
    h, hd = N_HEADS, HEAD_DIM
    qk = h * hd
    pd = pool_w.shape[1] * POOL_GROUP_DIM
    assert seq % DELTA_CHUNK == 0 and N_META <= DELTA_CHUNK
    m = batch * seq

    w = w_in[0]
    off_ba = 4 * qk
    off_p = off_ba + 2 * h
    off_g = off_p + pd
    w_main = jnp.concatenate([w[:, :off_ba], w[:, off_g:], w[:, off_p:off_g]], axis=1).astype(_bf16)
    w_ba = jnp.pad(w[:, off_ba:off_p], ((0, 0), (0, 128 - 2 * h))).astype(_bf16)
    col_ga, col_gb, col_p = off_ba, off_ba + d, off_ba + 2 * d
    assert col_ga % d == 0 and col_p % pd == 0

    x2 = x.reshape(m, d)
    nw = norm_mix_w[0].reshape(1, d)
    n_main = w_main.shape[1]
    tn = _largest_tile(n_main, 512)
    proj, ba = _in_proj(x2, nw, w_main, w_ba, _largest_tile(m, 1024), tn)
    meta_proj, meta_ba = _in_proj(meta_tokens, nw, w_main, w_ba, N_META, tn)
    pad = ((DELTA_CHUNK - N_META, 0), (0, 0))
    meta_proj_pad = jnp.pad(meta_proj, pad)
    meta_ba_pad = jnp.pad(meta_ba, pad)

    lane_pad = lambda a: jnp.pad(a.reshape(1, h), ((0, 0), (h, 128 - 2 * h)))
    conv_w3 = conv_w[0].T.reshape(3 * h, hd, CONV_K).transpose(0, 2, 1)
    o_gated = _delta_rule(proj, ba, meta_proj_pad, meta_ba_pad, conv_w3, lane_pad(a_log[0]),
                          lane_pad(dt_bias[0]), o_norm_w[0].reshape(1, hd), batch,
                          _largest_tile(seq, 512))

    h1 = _merge(proj, meta_proj, o_gated, x2, pool_w[0].astype(_bf16), pool_scale[0].reshape(1, pd),
                w_proj_a[0].astype(_bf16), w_proj_b[0].astype(_bf16), w_out[0].astype(_bf16),
                batch, _largest_tile(seq, 256), col_p, col_ga, col_gb)

    out = _ffn(h1, norm_ffn_w[0].reshape(1, d), w_ffn_gate[0].astype(_bf16),
               w_ffn_up[0].astype(_bf16), w_ffn_down[0].astype(_bf16),
               norm_final_w.reshape(1, d), _largest_tile(m, 512),
               _largest_tile(w_ffn_gate.shape[2], 512))
    return out.reshape(batch, seq, d)
```

```python
import functools

import jax
import jax.numpy as jnp
from jax import lax
from jax.experimental import pallas as pl
from jax.experimental.pallas import tpu as pltpu

N_META = 16
N_HEADS = 8
HEAD_DIM = 128
CONV_K = 4
POOL_WINDOWS = (2, 4, 8, 16)
POOL_GROUP_DIM = 256
RMS_EPS = 1e-6
L2_EPS = 1e-6

DELTA_CHUNK = 128
HIST_ROWS = 8

_V7X_VMEM_LIMIT = 56 * 1024 * 1024

_f32 = jnp.float32
_bf16 = jnp.bfloat16


def _dot(a, b):
    return jnp.dot(a, b, preferred_element_type=_f32)


def _dot_nt(a, b):
    return lax.dot_general(a, b, (((1,), (1,)), ((), ())), preferred_element_type=_f32)


def _sigmoid(x):
    return 1.0 / (1.0 + jnp.exp(-x))


def _silu(x):
    return x * _sigmoid(x)


def _softplus(x):
    return jnp.maximum(x, 0.0) + jnp.log1p(jnp.exp(-jnp.abs(x)))


def _rms_norm(x, w):
    return x * lax.rsqrt(jnp.mean(x * x, axis=-1, keepdims=True) + RMS_EPS) * w


def _in_proj_kernel(x_ref, nw_ref, wba_ref, w_ref, out_ref, ba_ref, hn_ref):
    @pl.when(pl.program_id(1) == 0)
    def _():
        hn = _rms_norm(x_ref[...], nw_ref[...]).astype(_bf16)
        hn_ref[...] = hn
        ba_ref[...] = _dot(hn, wba_ref[...])

    out_ref[...] = _dot(hn_ref[...], w_ref[...])


def _in_proj(x, norm_w, w_main, w_ba, tm, tn):
    m, d = x.shape
    n = w_main.shape[1]
    return pl.pallas_call(
        _in_proj_kernel,
        grid=(m // tm, n // tn),
        in_specs=[
            pl.BlockSpec((tm, d), lambda i, j: (i, 0)),
            pl.BlockSpec((1, d), lambda i, j: (0, 0)),
            pl.BlockSpec((d, 128), lambda i, j: (0, 0)),
            pl.BlockSpec((d, tn), lambda i, j: (0, j)),
        ],
        out_specs=[
            pl.BlockSpec((tm, tn), lambda i, j: (i, j)),
            pl.BlockSpec((tm, 128), lambda i, j: (i, 0)),
        ],
        out_shape=[
            jax.ShapeDtypeStruct((m, n), _f32),
            jax.ShapeDtypeStruct((m, 128), _f32),
        ],
        scratch_shapes=[pltpu.VMEM((tm, d), _bf16)],
        compiler_params=pltpu.CompilerParams(
            dimension_semantics=("arbitrary", "arbitrary"),
            vmem_limit_bytes=_V7X_VMEM_LIMIT),
        name="in_proj",
    )(x, norm_w, w_ba, w_main)


def _shift_rows(x, hist, s):
    xs = pltpu.roll(x, s, axis=0)
    hs = pltpu.roll(hist, s, axis=0)
    row = lax.broadcasted_iota(jnp.int32, hist.shape, 0)
    head = jnp.where(row < s, hs, xs[:HIST_ROWS])
    return jnp.concatenate([head, xs[HIST_ROWS:]], axis=0)


def _conv_silu(x, hist, cw):
    acc = x * cw[CONV_K - 1:CONV_K]
    for s in range(1, CONV_K):
        acc = acc + _shift_rows(x, hist, s) * cw[CONV_K - 1 - s:CONV_K - s]
    return _silu(acc)


def _l2_norm(x):
    return x * lax.rsqrt(jnp.sum(x * x, axis=-1, keepdims=True) + L2_EPS)


def _chunk_cumsum(g):
    pos = lax.broadcasted_iota(jnp.int32, g.shape, 0) % DELTA_CHUNK
    s = 1
    while s < DELTA_CHUNK:
        g = g + jnp.where(pos >= s, pltpu.roll(g, s, axis=0), 0.0)
        s *= 2
    return g


def _lane_column(x, lane_idx):
    lane = lax.broadcasted_iota(jnp.int32, x.shape, 1)
    col = jnp.sum(jnp.where(lane == lane_idx, x, 0.0), axis=1, keepdims=True)
    return jnp.broadcast_to(col, x.shape)


def _unit_lower_inverse(mats):
    c = mats[0].shape[0]
    row = lax.broadcasted_iota(jnp.int32, (c, c), 0)
    col = lax.broadcasted_iota(jnp.int32, (c, c), 1)
    eye = jnp.where(row == col, 1.0, 0.0)
    ps = [eye - a for a in mats]
    abs_ = [a.astype(_bf16) for a in mats]
    apows = [_dot(ab, ab) for ab in abs_]
    n = 2
    while n < c:
        apbs = [ap.astype(_bf16) for ap in apows]
        if 2 * n < c:
            both = [_dot(jnp.concatenate([p.astype(_bf16), apb], axis=0), apb)
                    for p, apb in zip(ps, apbs)]
            ps = [p + b[:c] for p, b in zip(ps, both)]
            apows = [b[c:] for b in both]
        else:
            ps = [p + _dot(p.astype(_bf16), apb) for p, apb in zip(ps, apbs)]
        n *= 2
    return ps


def _delta_chunks(probs, states):
    c = DELTA_CHUNK
    row = lax.broadcasted_iota(jnp.int32, (c, c), 0)
    col = lax.broadcasted_iota(jnp.int32, (c, c), 1)
    qn, kn, v, beta, gc = (list(t) for t in zip(*probs))
    n = len(probs)
    rng = range(n)
    decay = [jnp.exp(jnp.where(row >= col, gc[i] - gc[i].T, -jnp.inf)) for i in rng]
    qs = [qn[i] * (HEAD_DIM ** -0.5) for i in rng]
    kb = [kn[i].astype(_bf16) for i in rng]
    qk_kk = [_dot_nt(jnp.concatenate([qs[i].astype(_bf16), kb[i]], axis=0), kb[i]) for i in rng]
    qk = [(qk_kk[i][:c] * decay[i]).astype(_bf16) for i in rng]
    a_mat = [jnp.where(row > col, qk_kk[i][c:] * beta[i] * decay[i], 0.0) for i in rng]
    t_mat = _unit_lower_inverse(a_mat)
    eg = [jnp.exp(gc[i]) for i in rng]
    rhs = [jnp.concatenate([v[i] * beta[i], kn[i] * (beta[i] * eg[i])], axis=1).astype(_bf16)
           for i in rng]
    uw = [_dot(t_mat[i].astype(_bf16), rhs[i]) for i in rng]
    g_last = [gc[i][c - 1:c] for i in rng]
    k_dec_t = [(kn[i] * jnp.exp(g_last[i] - gc[i])).T.astype(_bf16) for i in rng]
    wq = [jnp.concatenate([uw[i][:, HEAD_DIM:], qs[i] * eg[i]], axis=0).astype(_bf16) for i in rng]
    ws_qs = [_dot(wq[i], states[i].astype(_bf16)) for i in rng]
    vb = [(uw[i][:, :HEAD_DIM] - ws_qs[i][:c]).astype(_bf16) for i in rng]
    outs = [ws_qs[i][c:] + _dot(qk[i], vb[i]) for i in rng]
    new_states = [states[i] * jnp.exp(g_last[i]) + _dot(k_dec_t[i], vb[i]) for i in rng]
    return outs, new_states


def _delta_kernel(q_ref, k_ref, v_ref, z_ref, ba_ref, mq_ref, mk_ref, mv_ref, mba_ref,
                  cwq_ref, cwk_ref, cwv_ref, alog_ref, dtb_ref, onw_ref,
                  o_ref, state_ref, hq_ref, hk_ref, hv_ref):
    c = DELTA_CHUNK
    hd = HEAD_DIM
    heads = range(N_HEADS)

    def prep(q_raw, k_raw, v_raw, ba, hq, hk, hv, valid):
        qc = _conv_silu(q_raw, hq, cwq_ref[...])
        kc = _conv_silu(k_raw, hk, cwk_ref[...])
        vc = _conv_silu(v_raw, hv, cwv_ref[...])
        beta_all = _sigmoid(ba)
        g_all = -jnp.exp(alog_ref[...]) * _softplus(ba + dtb_ref[...])
        if valid is not None:
            beta_all = jnp.where(valid, beta_all, 0.0)
            g_all = jnp.where(valid, g_all, 0.0)
        gc_all = _chunk_cumsum(g_all)
        probs = []
        for h in heads:
            sl = slice(h * hd, (h + 1) * hd)
            probs.append((_l2_norm(qc[:, sl]), _l2_norm(kc[:, sl]), vc[:, sl],
                          _lane_column(beta_all, h), _lane_column(gc_all, N_HEADS + h)))
        return probs

    @pl.when(pl.program_id(1) == 0)
    def _():
        mq, mk, mv = mq_ref[...], mk_ref[...], mv_ref[...]
        zeros = jnp.zeros((HIST_ROWS, N_HEADS * hd), _f32)
        is_meta = lax.broadcasted_iota(jnp.int32, (c, 128), 0) >= c - N_META
        probs = prep(mq, mk, mv, mba_ref[...], zeros, zeros, zeros, is_meta)
        _, states = _delta_chunks(probs, [jnp.zeros((hd, hd), _f32) for _ in heads])
        for h in heads:
            state_ref[h] = states[h]
        hq_ref[...] = mq[c - HIST_ROWS:]
        hk_ref[...] = mk[c - HIST_ROWS:]
        hv_ref[...] = mv[c - HIST_ROWS:]

    q_raw, k_raw, v_raw = q_ref[...], k_ref[...], v_ref[...]
    probs = prep(q_raw, k_raw, v_raw, ba_ref[...], hq_ref[...], hk_ref[...], hv_ref[...], None)
    hq_ref[...] = q_raw[c - HIST_ROWS:]
    hk_ref[...] = k_raw[c - HIST_ROWS:]
    hv_ref[...] = v_raw[c - HIST_ROWS:]
    outs, states = _delta_chunks(probs, [state_ref[h] for h in heads])
    for h in heads:
        state_ref[h] = states[h]
        o = outs[h]
        o = o * lax.rsqrt(jnp.mean(o * o, axis=-1, keepdims=True) + RMS_EPS)
        sl = slice(h * hd, (h + 1) * hd)
        o_ref[:, sl] = (o * onw_ref[...] * _silu(z_ref[:, sl])).astype(o_ref.dtype)


def _delta_rule(proj, ba, meta_proj, meta_ba, conv_w, alog_row, dtb_row, onw_row, batch):
    m = proj.shape[0]
    c = DELTA_CHUNK
    nblk = m // batch // c
    hd = HEAD_DIM
    w = N_HEADS * hd

    def tok(part):
        return pl.BlockSpec((c, w), lambda b, cb: (b * nblk + cb, part))

    def meta(part):
        return pl.BlockSpec((c, w), lambda b, cb: (0, part))

    def cw(part):
        return pl.BlockSpec((CONV_K, w), lambda b, cb: (0, part))

    row128 = pl.BlockSpec((1, 128), lambda b, cb: (0, 0))
    return pl.pallas_call(
        _delta_kernel,
        grid=(batch, nblk),
        in_specs=[
            tok(0), tok(1), tok(2), tok(3),
            pl.BlockSpec((c, 128), lambda b, cb: (b * nblk + cb, 0)),
            meta(0), meta(1), meta(2),
            pl.BlockSpec((c, 128), lambda b, cb: (0, 0)),
            cw(0), cw(1), cw(2),
            row128, row128, row128,
        ],
        out_specs=pl.BlockSpec((c, w), lambda b, cb: (b * nblk + cb, 0)),
        out_shape=jax.ShapeDtypeStruct((m, w), _bf16),
        scratch_shapes=[
            pltpu.VMEM((N_HEADS, hd, hd), _f32),
            pltpu.VMEM((HIST_ROWS, w), _f32),
            pltpu.VMEM((HIST_ROWS, w), _f32),
            pltpu.VMEM((HIST_ROWS, w), _f32),
        ],
        compiler_params=pltpu.CompilerParams(
            dimension_semantics=("arbitrary", "arbitrary"),
            vmem_limit_bytes=_V7X_VMEM_LIMIT),
        name="delta_rule",
    )(proj, proj, proj, proj, ba, meta_proj, meta_proj, meta_proj, meta_ba,
      conv_w, conv_w, conv_w, alog_row, dtb_row, onw_row)


def _merge_kernel(blocks_per_batch, p_ref, pprev_ref, pmeta_ref, ga_ref, gb_ref, o_ref, x_ref,
                  poolw_ref, pscale_ref, wpa_ref, wpb_ref, wout_ref, out_ref):
    first = (pl.program_id(0) % blocks_per_batch) == 0
    prev = jnp.where(first, pmeta_ref[...], pprev_ref[...])
    p = p_ref[...]
    ext = jnp.concatenate([prev, p], axis=0)
    gd = POOL_GROUP_DIM
    zb = []
    for gi, win in enumerate(POOL_WINDOWS):
        acc = ext[:, gi * gd:(gi + 1) * gd]
        s = 1
        while s < win:
            acc = acc + pltpu.roll(acc, s, axis=0)
            s *= 2
        pooled = acc[N_META:] * (1.0 / win) - p[:, gi * gd:(gi + 1) * gd]
        zb.append(_dot(pooled.astype(_bf16), poolw_ref[gi]))
    zb = (jnp.concatenate(zb, axis=1) * pscale_ref[...]).astype(_bf16)
    y_b = _dot(zb, wpb_ref[...])
    y_a = _dot(o_ref[...], wpa_ref[...])
    merged = _sigmoid(ga_ref[...]) * y_a + _sigmoid(gb_ref[...]) * y_b
    out_ref[...] = x_ref[...] + _dot(merged.astype(_bf16), wout_ref[...])


def _merge(proj, meta_proj, o_gated, x, pool_w, pool_scale, w_proj_a, w_proj_b, w_out,
           batch, tm, col_p, col_ga, col_gb):
    m, d = x.shape
    pd = pool_w.shape[0] * POOL_GROUP_DIM
    vd = o_gated.shape[1]
    blocks_per_batch = m // batch // tm
    sub = tm // N_META

    def const(shape):
        return pl.BlockSpec(shape, lambda i: (0,) * len(shape), pipeline_mode=pl.Buffered(1))

    return pl.pallas_call(
        functools.partial(_merge_kernel, blocks_per_batch),
        grid=(m // tm,),
        in_specs=[
            pl.BlockSpec((tm, pd), lambda i: (i, col_p // pd)),
            pl.BlockSpec((N_META, pd), lambda i: (jnp.maximum(i * sub - 1, 0), col_p // pd)),
            pl.BlockSpec((N_META, pd), lambda i: (0, col_p // pd)),
            pl.BlockSpec((tm, d), lambda i: (i, col_ga // d)),
            pl.BlockSpec((tm, d), lambda i: (i, col_gb // d)),
            pl.BlockSpec((tm, vd), lambda i: (i, 0)),
            pl.BlockSpec((tm, d), lambda i: (i, 0)),
            const(pool_w.shape), const((1, pd)), const(w_proj_a.shape), const(w_proj_b.shape),
            const(w_out.shape),
        ],
        out_specs=pl.BlockSpec((tm, d), lambda i: (i, 0)),
        out_shape=jax.ShapeDtypeStruct((m, d), _f32),
        compiler_params=pltpu.CompilerParams(
            dimension_semantics=("arbitrary",),
            vmem_limit_bytes=_V7X_VMEM_LIMIT),
        name="merge",
    )(proj, proj, meta_proj, proj, proj, o_gated, x, pool_w, pool_scale, w_proj_a, w_proj_b, w_out)


def _ffn_kernel(h_ref, nw_ref, wg_ref, wu_ref, wd_ref, fw_ref, out_ref, hn_ref, acc_ref):
    f = pl.program_id(1)

    @pl.when(f == 0)
    def _():
        hn_ref[...] = _rms_norm(h_ref[...], nw_ref[...]).astype(_bf16)
        acc_ref[...] = jnp.zeros_like(acc_ref)

    hn = hn_ref[...]
    act = (_silu(_dot(hn, wg_ref[...])) * _dot(hn, wu_ref[...])).astype(_bf16)
    acc_ref[...] += _dot(act, wd_ref[...])

    @pl.when(f == pl.num_programs(1) - 1)
    def _():
        out_ref[...] = _rms_norm(h_ref[...] + acc_ref[...], fw_ref[...])


def _ffn(h, norm_w, w_gate, w_up, w_down, final_w, tm, tf):
    m, d = h.shape
    f = w_gate.shape[1]
    return pl.pallas_call(
        _ffn_kernel,
        grid=(m // tm, f // tf),
        in_specs=[
            pl.BlockSpec((tm, d), lambda i, j: (i, 0)),
            pl.BlockSpec((1, d), lambda i, j: (0, 0)),
            pl.BlockSpec((d, tf), lambda i, j: (0, j)),
            pl.BlockSpec((d, tf), lambda i, j: (0, j)),
            pl.BlockSpec((tf, d), lambda i, j: (j, 0)),
            pl.BlockSpec((1, d), lambda i, j: (0, 0)),
        ],
        out_specs=pl.BlockSpec((tm, d), lambda i, j: (i, 0)),
        out_shape=jax.ShapeDtypeStruct((m, d), _f32),
        scratch_shapes=[pltpu.VMEM((tm, d), _bf16), pltpu.VMEM((tm, d), _f32)],
        compiler_params=pltpu.CompilerParams(
            dimension_semantics=("arbitrary", "arbitrary"),
            vmem_limit_bytes=_V7X_VMEM_LIMIT),
        name="ffn",
    )(h, norm_w, w_gate, w_up, w_down, final_w)


def _largest_tile(n, cap):
    t = cap
    while n % t:
        t //= 2
    return t


def kernel(x, meta_tokens, norm_mix_w, w_in, conv_w, a_log, dt_bias, o_norm_w, w_proj_a, pool_w,
           pool_scale, w_proj_b, w_out, norm_ffn_w, w_ffn_gate, w_ffn_up, w_ffn_down, norm_final_w):
    assert w_in.shape[0] == 1, "single layer block"
    batch, seq, d = x.shape
    h, hd = N_HEADS, HEAD_DIM
    qk = h * hd
    pd = pool_w.shape[1] * POOL_GROUP_DIM
    assert seq % DELTA_CHUNK == 0 and N_META <= DELTA_CHUNK
    m = batch * seq

    w = w_in[0]
    off_ba = 4 * qk
    off_p = off_ba + 2 * h
    off_g = off_p + pd
    w_main = jnp.concatenate([w[:, :off_ba], w[:, off_g:], w[:, off_p:off_g]], axis=1).astype(_bf16)
    w_ba = jnp.pad(w[:, off_ba:off_p], ((0, 0), (0, 128 - 2 * h))).astype(_bf16)
    col_ga, col_gb, col_p = off_ba, off_ba + d, off_ba + 2 * d
    assert col_ga % d == 0 and col_p % pd == 0

    x2 = x.reshape(m, d)
    nw = norm_mix_w[0].reshape(1, d)
    n_main = w_main.shape[1]
    tn = _largest_tile(n_main, 512)
    proj, ba = _in_proj(x2, nw, w_main, w_ba, _largest_tile(m, 1024), tn)
    meta_proj, meta_ba = _in_proj(meta_tokens, nw, w_main, w_ba, N_META, tn)
    pad = ((DELTA_CHUNK - N_META, 0), (0, 0))
    meta_proj_pad = jnp.pad(meta_proj, pad)
    meta_ba_pad = jnp.pad(meta_ba, pad)

    lane_pad = lambda a: jnp.pad(a.reshape(1, h), ((0, 0), (h, 128 - 2 * h)))
    o_gated = _delta_rule(proj, ba, meta_proj_pad, meta_ba_pad, conv_w[0], lane_pad(a_log[0]),
                          lane_pad(dt_bias[0]), o_norm_w[0].reshape(1, hd), batch)

    h1 = _merge(proj, meta_proj, o_gated, x2, pool_w[0].astype(_bf16), pool_scale[0].reshape(1, pd),
                w_proj_a[0].astype(_bf16), w_proj_b[0].astype(_bf16), w_out[0].astype(_bf16),
                batch, _largest_tile(seq, 256), col_p, col_ga, col_gb)

    out = _ffn(h1, norm_ffn_w[0].reshape(1, d), w_ffn_gate[0].astype(_bf16),
               w_ffn_up[0].astype(_bf16), w_ffn_down[0].astype(_bf16),
               norm_final_w.reshape(1, d), _largest_tile(m, 512),
               _largest_tile(w_ffn_gate.shape[2], 512))
    return out.reshape(batch, seq, d)
```

```python
import functools

import jax
import jax.numpy as jnp
from jax import lax
from jax.experimental import pallas as pl
from jax.experimental.pallas import tpu as pltpu

N_META = 16
N_HEADS = 8
HEAD_DIM = 128
CONV_K = 4
POOL_WINDOWS = (2, 4, 8, 16)
POOL_GROUP_DIM = 256
RMS_EPS = 1e-6
L2_EPS = 1e-6

DELTA_CHUNK = 128
INV_BASE = 16
HIST_ROWS = 8

_V7X_VMEM_LIMIT = 56 * 1024 * 1024

_f32 = jnp.float32
_bf16 = jnp.bfloat16


def _dot(a, b):
    return jnp.dot(a, b, preferred_element_type=_f32)


def _dot_nt(a, b):
    return lax.dot_general(a, b, (((1,), (1,)), ((), ())), preferred_element_type=_f32)


def _sigmoid(x):
    return 1.0 / (1.0 + jnp.exp(-x))


def _silu(x):
    return x * _sigmoid(x)


def _softplus(x):
    return jnp.maximum(x, 0.0) + jnp.log1p(jnp.exp(-jnp.abs(x)))


def _rms_norm(x, w):
    return x * lax.rsqrt(jnp.mean(x * x, axis=-1, keepdims=True) + RMS_EPS) * w


def _in_proj_kernel(n_lin, n_sig, x_ref, nw_ref, wba_ref, w_ref, lin_ref, act_ref, ba_ref, hn_ref):
    j = pl.program_id(1)

    @pl.when(j == 0)
    def _():
        hn = _rms_norm(x_ref[...], nw_ref[...]).astype(_bf16)
        hn_ref[...] = hn
        ba_ref[...] = _dot(hn, wba_ref[...])

    @pl.when(j < n_lin)
    def _():
        lin_ref[...] = _dot(hn_ref[...], w_ref[...])

    @pl.when(jnp.logical_and(j >= n_lin, j < n_lin + n_sig))
    def _():
        act_ref[...] = _sigmoid(_dot(hn_ref[...], w_ref[...])).astype(act_ref.dtype)

    @pl.when(j >= n_lin + n_sig)
    def _():
        act_ref[...] = _silu(_dot(hn_ref[...], w_ref[...])).astype(act_ref.dtype)


def _in_proj(x, norm_w, w_main, w_ba, tm, tn, lin_cols, sig_cols):
    m, d = x.shape
    n = w_main.shape[1]
    assert lin_cols % tn == 0 and sig_cols % tn == 0 and n % tn == 0
    n_lin, n_sig = lin_cols // tn, sig_cols // tn
    return pl.pallas_call(
        functools.partial(_in_proj_kernel, n_lin, n_sig),
        grid=(m // tm, n // tn),
        in_specs=[
            pl.BlockSpec((tm, d), lambda i, j: (i, 0)),
            pl.BlockSpec((1, d), lambda i, j: (0, 0)),
            pl.BlockSpec((d, 128), lambda i, j: (0, 0)),
            pl.BlockSpec((d, tn), lambda i, j: (0, j)),
        ],
        out_specs=[
            pl.BlockSpec((tm, tn), lambda i, j: (i, jnp.minimum(j, n_lin - 1))),
            pl.BlockSpec((tm, tn), lambda i, j: (i, jnp.maximum(j - n_lin, 0))),
            pl.BlockSpec((tm, 128), lambda i, j: (i, 0)),
        ],
        out_shape=[
            jax.ShapeDtypeStruct((m, lin_cols), _f32),
            jax.ShapeDtypeStruct((m, n - lin_cols), _bf16),
            jax.ShapeDtypeStruct((m, 128), _f32),
        ],
        scratch_shapes=[pltpu.VMEM((tm, d), _bf16)],
        compiler_params=pltpu.CompilerParams(
            dimension_semantics=("arbitrary", "arbitrary"),
            vmem_limit_bytes=_V7X_VMEM_LIMIT),
        name="in_proj",
    )(x, norm_w, w_ba, w_main)


def _conv_silu(x, hist, cw):
    assert CONV_K == 4
    ext = jnp.concatenate([hist, x], axis=0)
    ext1 = pltpu.roll(ext, 1, axis=0)
    old = ext * cw[1:2] + ext1 * cw[0:1]
    acc = ext * cw[3:4] + ext1 * cw[2:3] + pltpu.roll(old, 2, axis=0)
    return _silu(acc[HIST_ROWS:])


def _l2_norm(x):
    return x * lax.rsqrt(jnp.sum(x * x, axis=-1, keepdims=True) + L2_EPS)


def _chunk_cumsum(g):
    pos = lax.broadcasted_iota(jnp.int32, g.shape, 0) % DELTA_CHUNK
    s = 1
    while s < DELTA_CHUNK:
        g = g + jnp.where(pos >= s, pltpu.roll(g, s, axis=0), 0.0)
        s *= 2
    return g


def _lane_column(x, lane_idx):
    lane = lax.broadcasted_iota(jnp.int32, x.shape, 1)
    col = jnp.sum(jnp.where(lane == lane_idx, x, 0.0), axis=1, keepdims=True)
    return jnp.broadcast_to(col, x.shape)


def _unit_lower_inverse(mats):
    c = mats[0].shape[0]
    row = lax.broadcasted_iota(jnp.int32, (c, c), 0)
    col = lax.broadcasted_iota(jnp.int32, (c, c), 1)
    apart = row ^ col
    eye = jnp.where(row == col, 1.0, 0.0)
    diag = [jnp.where(apart < INV_BASE, a, 0.0) for a in mats]
    ts = [eye - a for a in diag]
    apows = [a.astype(_bf16) for a in diag]
    apows = [_dot(ab, ab) for ab in apows]
    n = 2
    while n < INV_BASE:
        apbs = [ap.astype(_bf16) for ap in apows]
        if 2 * n < INV_BASE:
            both = [_dot(jnp.concatenate([t.astype(_bf16), apb], axis=0), apb)
                    for t, apb in zip(ts, apbs)]
            ts = [t + b[:c] for t, b in zip(ts, both)]
            apows = [b[c:] for b in both]
        else:
            ts = [t + _dot(t.astype(_bf16), apb) for t, apb in zip(ts, apbs)]
        n *= 2
    size = INV_BASE
    while size < c:
        lower_left = jnp.logical_and(apart >= size, apart < 2 * size)
        tbs = [t.astype(_bf16) for t in ts]
        xs = [_dot(jnp.where(lower_left, a, 0.0).astype(_bf16), tb) for a, tb in zip(mats, tbs)]
        ts = [t - _dot(tb, x.astype(_bf16)) for t, tb, x in zip(ts, tbs, xs)]
        size *= 2
    return ts


def _delta_chunks(probs, states):
    c = DELTA_CHUNK
    row = lax.broadcasted_iota(jnp.int32, (c, c), 0)
    col = lax.broadcasted_iota(jnp.int32, (c, c), 1)
    qn, kn, v, beta, gc = (list(t) for t in zip(*probs))
    n = len(probs)
    rng = range(n)
    decay = [jnp.exp(jnp.where(row >= col, gc[i] - gc[i].T, -jnp.inf)) for i in rng]
    qs = [qn[i] * (HEAD_DIM ** -0.5) for i in rng]
    kb = [kn[i].astype(_bf16) for i in rng]
    qk_kk = [_dot_nt(jnp.concatenate([qs[i].astype(_bf16), kb[i]], axis=0), kb[i]) for i in rng]
    qk = [(qk_kk[i][:c] * decay[i]).astype(_bf16) for i in rng]
    a_mat = [jnp.where(row > col, qk_kk[i][c:] * beta[i] * decay[i], 0.0) for i in rng]
    t_mat = _unit_lower_inverse(a_mat)
    eg = [jnp.exp(gc[i]) for i in rng]
    rhs = [jnp.concatenate([v[i] * beta[i], kn[i] * (beta[i] * eg[i])], axis=1).astype(_bf16)
           for i in rng]
    uw = [_dot(t_mat[i].astype(_bf16), rhs[i]) for i in rng]
    g_last = [gc[i][c - 1:c] for i in rng]
    k_dec_t = [(kn[i] * jnp.exp(g_last[i] - gc[i])).T.astype(_bf16) for i in rng]
    wq = [jnp.concatenate([uw[i][:, HEAD_DIM:], qs[i] * eg[i]], axis=0).astype(_bf16) for i in rng]
    ws_qs = [_dot(wq[i], states[i].astype(_bf16)) for i in rng]
    vb = [(uw[i][:, :HEAD_DIM] - ws_qs[i][:c]).astype(_bf16) for i in rng]
    outs = [ws_qs[i][c:] + _dot(qk[i], vb[i]) for i in rng]
    new_states = [states[i] * jnp.exp(g_last[i]) + _dot(k_dec_t[i], vb[i]) for i in rng]
    return outs, new_states


def _delta_kernel(q_ref, k_ref, v_ref, zs_ref, ba_ref, mq_ref, mk_ref, mv_ref, mba_ref,
                  cwq_ref, cwk_ref, cwv_ref, alog_ref, dtb_ref, onw_ref,
                  o_ref, state_ref, hq_ref, hk_ref, hv_ref):
    c = DELTA_CHUNK
    hd = HEAD_DIM
    heads = range(N_HEADS)

    def prep(q_raw, k_raw, v_raw, ba, hq, hk, hv, valid):
        qc = _conv_silu(q_raw, hq, cwq_ref[...])
        kc = _conv_silu(k_raw, hk, cwk_ref[...])
        vc = _conv_silu(v_raw, hv, cwv_ref[...])
        beta_all = _sigmoid(ba)
        g_all = -jnp.exp(alog_ref[...]) * _softplus(ba + dtb_ref[...])
        if valid is not None:
            beta_all = jnp.where(valid, beta_all, 0.0)
            g_all = jnp.where(valid, g_all, 0.0)
        gc_all = _chunk_cumsum(g_all)
        probs = []
        for h in heads:
            sl = slice(h * hd, (h + 1) * hd)
            probs.append((_l2_norm(qc[:, sl]), _l2_norm(kc[:, sl]), vc[:, sl],
                          _lane_column(beta_all, h), _lane_column(gc_all, N_HEADS + h)))
        return probs

    @pl.when(pl.program_id(1) == 0)
    def _():
        mq, mk, mv = mq_ref[...], mk_ref[...], mv_ref[...]
        zeros = jnp.zeros((HIST_ROWS, N_HEADS * hd), _f32)
        is_meta = lax.broadcasted_iota(jnp.int32, (c, 128), 0) >= c - N_META
        probs = prep(mq, mk, mv, mba_ref[...], zeros, zeros, zeros, is_meta)
        _, states = _delta_chunks(probs, [jnp.zeros((hd, hd), _f32) for _ in heads])
        for h in heads:
            state_ref[h] = states[h]
        hq_ref[...] = mq[c - HIST_ROWS:]
        hk_ref[...] = mk[c - HIST_ROWS:]
        hv_ref[...] = mv[c - HIST_ROWS:]

    q_raw, k_raw, v_raw = q_ref[...], k_ref[...], v_ref[...]
    probs = prep(q_raw, k_raw, v_raw, ba_ref[...], hq_ref[...], hk_ref[...], hv_ref[...], None)
    hq_ref[...] = q_raw[c - HIST_ROWS:]
    hk_ref[...] = k_raw[c - HIST_ROWS:]
    hv_ref[...] = v_raw[c - HIST_ROWS:]
    outs, states = _delta_chunks(probs, [state_ref[h] for h in heads])
    for h in heads:
        state_ref[h] = states[h]
        o = outs[h]
        o = o * lax.rsqrt(jnp.mean(o * o, axis=-1, keepdims=True) + RMS_EPS)
        sl = slice(h * hd, (h + 1) * hd)
        o_ref[:, sl] = (o * onw_ref[...] * zs_ref[:, sl].astype(_f32)).astype(o_ref.dtype)


def _delta_rule(lin, act, ba, meta_lin, meta_ba, conv_w, alog_row, dtb_row, onw_row, batch, zs_part):
    m = lin.shape[0]
    c = DELTA_CHUNK
    nblk = m // batch // c
    hd = HEAD_DIM
    w = N_HEADS * hd

    def tok(part):
        return pl.BlockSpec((c, w), lambda b, cb: (b * nblk + cb, part))

    def meta(part):
        return pl.BlockSpec((c, w), lambda b, cb: (0, part))

    def cw(part):
        return pl.BlockSpec((CONV_K, w), lambda b, cb: (0, part))

    row128 = pl.BlockSpec((1, 128), lambda b, cb: (0, 0))
    return pl.pallas_call(
        _delta_kernel,
        grid=(batch, nblk),
        in_specs=[
            tok(0), tok(1), tok(2), tok(zs_part),
            pl.BlockSpec((c, 128), lambda b, cb: (b * nblk + cb, 0)),
            meta(0), meta(1), meta(2),
            pl.BlockSpec((c, 128), lambda b, cb: (0, 0)),
            cw(0), cw(1), cw(2),
            row128, row128, row128,
        ],
        out_specs=pl.BlockSpec((c, w), lambda b, cb: (b * nblk + cb, 0)),
        out_shape=jax.ShapeDtypeStruct((m, w), _bf16),
        scratch_shapes=[
            pltpu.VMEM((N_HEADS, hd, hd), _f32),
            pltpu.VMEM((HIST_ROWS, w), _f32),
            pltpu.VMEM((HIST_ROWS, w), _f32),
            pltpu.VMEM((HIST_ROWS, w), _f32),
        ],
        compiler_params=pltpu.CompilerParams(
            dimension_semantics=("arbitrary", "arbitrary"),
            vmem_limit_bytes=_V7X_VMEM_LIMIT),
        name="delta_rule",
    )(lin, lin, lin, act, ba, meta_lin, meta_lin, meta_lin, meta_ba,
      conv_w, conv_w, conv_w, alog_row, dtb_row, onw_row)


def _merge_kernel(blocks_per_batch, p_ref, pprev_ref, pmeta_ref, sa_ref, sb_ref, o_ref, x_ref,
                  poolw_ref, pscale_ref, wpa_ref, wpb_ref, wout_ref, nw_ref, out_ref, hn_ref):
    first = (pl.program_id(0) % blocks_per_batch) == 0
    prev = jnp.where(first, pmeta_ref[...], pprev_ref[...])
    p = p_ref[...]
    ext = jnp.concatenate([prev, p], axis=0)
    gd = POOL_GROUP_DIM
    zb = []
    for gi, win in enumerate(POOL_WINDOWS):
        acc = ext[:, gi * gd:(gi + 1) * gd]
        s = 1
        while s < win:
            acc = acc + pltpu.roll(acc, s, axis=0)
            s *= 2
        pooled = acc[N_META:] * (1.0 / win) - p[:, gi * gd:(gi + 1) * gd]
        zb.append(_dot(pooled.astype(_bf16), poolw_ref[gi]))
    zb = (jnp.concatenate(zb, axis=1) * pscale_ref[...]).astype(_bf16)
    y_b = _dot(zb, wpb_ref[...])
    y_a = _dot(o_ref[...], wpa_ref[...])
    merged = sa_ref[...].astype(_f32) * y_a + sb_ref[...].astype(_f32) * y_b
    h1 = x_ref[...] + _dot(merged.astype(_bf16), wout_ref[...])
    out_ref[...] = h1
    hn_ref[...] = _rms_norm(h1, nw_ref[...]).astype(hn_ref.dtype)


def _merge(lin, act, meta_lin, o_gated, x, pool_w, pool_scale, w_proj_a, w_proj_b, w_out, ffn_norm_w,
           batch, tm, p_part):
    m, d = x.shape
    pd = pool_w.shape[0] * POOL_GROUP_DIM
    vd = o_gated.shape[1]
    blocks_per_batch = m // batch // tm
    sub = tm // N_META

    def const(shape):
        return pl.BlockSpec(shape, lambda i: (0,) * len(shape), pipeline_mode=pl.Buffered(1))

    return pl.pallas_call(
        functools.partial(_merge_kernel, blocks_per_batch),
        grid=(m // tm,),
        in_specs=[
            pl.BlockSpec((tm, pd), lambda i: (i, p_part)),
            pl.BlockSpec((N_META, pd), lambda i: (jnp.maximum(i * sub - 1, 0), p_part)),
            pl.BlockSpec((N_META, pd), lambda i: (0, p_part)),
            pl.BlockSpec((tm, d), lambda i: (i, 0)),
            pl.BlockSpec((tm, d), lambda i: (i, 1)),
            pl.BlockSpec((tm, vd), lambda i: (i, 0)),
            pl.BlockSpec((tm, d), lambda i: (i, 0)),
            const(pool_w.shape), const((1, pd)), const(w_proj_a.shape), const(w_proj_b.shape),
            const(w_out.shape), const((1, d)),
        ],
        out_specs=[pl.BlockSpec((tm, d), lambda i: (i, 0)), pl.BlockSpec((tm, d), lambda i: (i, 0))],
        out_shape=[jax.ShapeDtypeStruct((m, d), _f32), jax.ShapeDtypeStruct((m, d), _bf16)],
        compiler_params=pltpu.CompilerParams(
            dimension_semantics=("arbitrary",),
            vmem_limit_bytes=_V7X_VMEM_LIMIT),
        name="merge",
    )(lin, lin, meta_lin, act, act, o_gated, x, pool_w, pool_scale, w_proj_a, w_proj_b, w_out,
      ffn_norm_w)


def _ffn_kernel(h_hbm, hn_ref, wg_ref, wu_ref, wd_ref, fw_ref, out_ref, h_buf, h_sem):
    i = pl.program_id(0)
    f = pl.program_id(1)
    tm = out_ref.shape[0]

    def residual_copy():
        rows = pl.ds(pl.multiple_of(i * tm, tm), tm)
        return pltpu.make_async_copy(h_hbm.at[rows], h_buf, h_sem)

    @pl.when(f == 0)
    def _():
        residual_copy().start()
        out_ref[...] = jnp.zeros_like(out_ref)

    d = out_ref.shape[1]
    rc, dc = min(tm, 512), min(d, 512)
    for r0 in range(0, tm, rc):
        hn = hn_ref[r0:r0 + rc, :]
        act = (_silu(_dot(hn, wg_ref[...])) * _dot(hn, wu_ref[...])).astype(_bf16)
        for c0 in range(0, d, dc):
            out_ref[r0:r0 + rc, c0:c0 + dc] += _dot(act, wd_ref[:, c0:c0 + dc])

    @pl.when(f == pl.num_programs(1) - 1)
    def _():
        residual_copy().wait()
        out_ref[...] = _rms_norm(h_buf[...] + out_ref[...], fw_ref[...])


def _ffn(h, hn, w_gate, w_up, w_down, final_w, tm, tf):
    m, d = h.shape
    f = w_gate.shape[1]
    return pl.pallas_call(
        _ffn_kernel,
        grid=(m // tm, f // tf),
        in_specs=[
            pl.BlockSpec(memory_space=pl.ANY),
            pl.BlockSpec((tm, d), lambda i, j: (i, 0)),
            pl.BlockSpec((d, tf), lambda i, j: (0, j)),
            pl.BlockSpec((d, tf), lambda i, j: (0, j)),
            pl.BlockSpec((tf, d), lambda i, j: (j, 0)),
            pl.BlockSpec((1, d), lambda i, j: (0, 0)),
        ],
        out_specs=pl.BlockSpec((tm, d), lambda i, j: (i, 0)),
        out_shape=jax.ShapeDtypeStruct((m, d), _f32),
        scratch_shapes=[pltpu.VMEM((tm, d), _f32), pltpu.SemaphoreType.DMA(())],
        compiler_params=pltpu.CompilerParams(
            dimension_semantics=("arbitrary", "arbitrary"),
            vmem_limit_bytes=_V7X_VMEM_LIMIT),
        name="ffn",
    )(h, hn, w_gate, w_up, w_down, final_w)


def _largest_tile(n, cap):
    t = cap
    while n % t:
        t //= 2
    return t


def kernel(x, meta_tokens, norm_mix_w, w_in, conv_w, a_log, dt_bias, o_norm_w, w_proj_a, pool_w,
           pool_scale, w_proj_b, w_out, norm_ffn_w, w_ffn_gate, w_ffn_up, w_ffn_down, norm_final_w):
    assert w_in.shape[0] == 1, "single layer block"
    batch, seq, d = x.shape
    h, hd = N_HEADS, HEAD_DIM
    qk = h * hd
    pd = pool_w.shape[1] * POOL_GROUP_DIM
    assert seq % DELTA_CHUNK == 0 and N_META <= DELTA_CHUNK
    m = batch * seq

    w = w_in[0]
    off_z = 3 * qk
    off_ba = 4 * qk
    off_p = off_ba + 2 * h
    off_g = off_p + pd
    w_main = jnp.concatenate([w[:, :off_z], w[:, off_p:off_g], w[:, off_g:], w[:, off_z:off_ba]],
                             axis=1).astype(_bf16)
    w_ba = jnp.pad(w[:, off_ba:off_p], ((0, 0), (0, 128 - 2 * h))).astype(_bf16)
    lin_cols, sig_cols = off_z + pd, 2 * d
    assert pd == qk and sig_cols % qk == 0
    p_part, zs_part = off_z // pd, sig_cols // qk

    x2 = x.reshape(m, d)
    nw = norm_mix_w[0].reshape(1, d)
    tn = qk
    lin, act, ba = _in_proj(x2, nw, w_main, w_ba, _largest_tile(m, 1024), tn, lin_cols, sig_cols)
    meta_lin, _, meta_ba = _in_proj(meta_tokens, nw, w_main, w_ba, N_META, tn, lin_cols, sig_cols)
    pad = ((DELTA_CHUNK - N_META, 0), (0, 0))
    meta_lin_pad = jnp.pad(meta_lin, pad)
    meta_ba_pad = jnp.pad(meta_ba, pad)

    lane_pad = lambda a: jnp.pad(a.reshape(1, h), ((0, 0), (h, 128 - 2 * h)))
    o_gated = _delta_rule(lin, act, ba, meta_lin_pad, meta_ba_pad, conv_w[0], lane_pad(a_log[0]),
                          lane_pad(dt_bias[0]), o_norm_w[0].reshape(1, hd), batch, zs_part)

    h1, hn2 = _merge(lin, act, meta_lin, o_gated, x2, pool_w[0].astype(_bf16),
                     pool_scale[0].reshape(1, pd), w_proj_a[0].astype(_bf16),
                     w_proj_b[0].astype(_bf16), w_out[0].astype(_bf16), norm_ffn_w[0].reshape(1, d),
                     batch, _largest_tile(seq, 256), p_part)

    out = _ffn(h1, hn2, w_ffn_gate[0].astype(_bf16), w_ffn_up[0].astype(_bf16),
               w_ffn_down[0].astype(_bf16), norm_final_w.reshape(1, d), _largest_tile(m, 1024),
               _largest_tile(w_ffn_gate.shape[2], 512))
    return out.reshape(batch, seq, d)
```

```python
import functools

import jax
import jax.numpy as jnp
from jax import lax
from jax.experimental import pallas as pl
from jax.experimental.pallas import tpu as pltpu

N_META = 16
N_HEADS = 8
HEAD_DIM = 128
CONV_K = 4
POOL_WINDOWS = (2, 4, 8, 16)
POOL_GROUP_DIM = 256
RMS_EPS = 1e-6
L2_EPS = 1e-6

DELTA_CHUNK = 128
INV_BASE = 16
HIST_ROWS = 8

_V7X_VMEM_LIMIT = 56 * 1024 * 1024

_f32 = jnp.float32
_bf16 = jnp.bfloat16


def _dot(a, b):
    return jnp.dot(a, b, preferred_element_type=_f32)


def _dot_nt(a, b):
    return lax.dot_general(a, b, (((1,), (1,)), ((), ())), preferred_element_type=_f32)


def _sigmoid(x):
    return 1.0 / (1.0 + jnp.exp(-x))


def _silu(x):
    return x * _sigmoid(x)


def _softplus(x):
    return jnp.maximum(x, 0.0) + jnp.log1p(jnp.exp(-jnp.abs(x)))


def _rms_norm(x, w):
    return x * lax.rsqrt(jnp.mean(x * x, axis=-1, keepdims=True) + RMS_EPS) * w


def _w_prep_kernel(w_ref, o_ref):
    o_ref[...] = w_ref[...].astype(o_ref.dtype)


def _w_prep(w_t, row_offsets, tn, tk):
    n_src, k = w_t.shape
    assert all(off % 8 == 0 and off + tn <= n_src for off in row_offsets)

    def row_off(j):
        tile = 0
        for jj, o in enumerate(row_offsets):
            tile = jnp.where(j == jj, o // 8, tile)
        return tile * 8

    return pl.pallas_call(
        _w_prep_kernel,
        grid=(len(row_offsets), k // tk),
        in_specs=[pl.BlockSpec((pl.Element(tn), pl.Element(tk)),
                               lambda j, c: (row_off(j), c * tk))],
        out_specs=pl.BlockSpec((tn, tk), lambda j, c: (j, c)),
        out_shape=jax.ShapeDtypeStruct((len(row_offsets) * tn, k), _bf16),
        compiler_params=pltpu.CompilerParams(
            dimension_semantics=("arbitrary", "arbitrary"),
            vmem_limit_bytes=_V7X_VMEM_LIMIT),
        name="w_prep",
    )(w_t)


def _in_proj_kernel(n_lin, n_sig, x_ref, nw_ref, wba_ref, w_ref, lin_ref, act_ref, ba_ref, hn_ref):
    j = pl.program_id(1)

    @pl.when(j == 0)
    def _():
        hn = _rms_norm(x_ref[...], nw_ref[...]).astype(_bf16)
        hn_ref[...] = hn
        ba_ref[...] = _dot_nt(hn, wba_ref[...])

    @pl.when(j < n_lin)
    def _():
        lin_ref[...] = _dot_nt(hn_ref[...], w_ref[...])

    @pl.when(jnp.logical_and(j >= n_lin, j < n_lin + n_sig))
    def _():
        act_ref[...] = _sigmoid(_dot_nt(hn_ref[...], w_ref[...])).astype(act_ref.dtype)

    @pl.when(j >= n_lin + n_sig)
    def _():
        act_ref[...] = _silu(_dot_nt(hn_ref[...], w_ref[...])).astype(act_ref.dtype)


def _in_proj(x, norm_w, w_main_t, w_ba_t, tm, tn, lin_cols, sig_cols):
    m, d = x.shape
    n = w_main_t.shape[0]
    assert lin_cols % tn == 0 and sig_cols % tn == 0 and n % tn == 0
    n_lin, n_sig = lin_cols // tn, sig_cols // tn
    return pl.pallas_call(
        functools.partial(_in_proj_kernel, n_lin, n_sig),
        grid=(m // tm, n // tn),
        in_specs=[
            pl.BlockSpec((tm, d), lambda i, j: (i, 0)),
            pl.BlockSpec((1, d), lambda i, j: (0, 0)),
            pl.BlockSpec((128, d), lambda i, j: (0, 0)),
            pl.BlockSpec((tn, d), lambda i, j: (j, 0)),
        ],
        out_specs=[
            pl.BlockSpec((tm, tn), lambda i, j: (i, jnp.minimum(j, n_lin - 1))),
            pl.BlockSpec((tm, tn), lambda i, j: (i, jnp.maximum(j - n_lin, 0))),
            pl.BlockSpec((tm, 128), lambda i, j: (i, 0)),
        ],
        out_shape=[
            jax.ShapeDtypeStruct((m, lin_cols), _f32),
            jax.ShapeDtypeStruct((m, n - lin_cols), _bf16),
            jax.ShapeDtypeStruct((m, 128), _f32),
        ],
        scratch_shapes=[pltpu.VMEM((tm, d), _bf16)],
        compiler_params=pltpu.CompilerParams(
            dimension_semantics=("arbitrary", "arbitrary"),
            vmem_limit_bytes=_V7X_VMEM_LIMIT),
        name="in_proj",
    )(x, norm_w, w_ba_t, w_main_t)


def _conv_silu(x, hist, cw):
    assert CONV_K == 4
    ext = jnp.concatenate([hist, x], axis=0)
    ext1 = pltpu.roll(ext, 1, axis=0)
    old = ext * cw[1:2] + ext1 * cw[0:1]
    acc = ext * cw[3:4] + ext1 * cw[2:3] + pltpu.roll(old, 2, axis=0)
    return _silu(acc[HIST_ROWS:])


def _l2_norm(x):
    return x * lax.rsqrt(jnp.sum(x * x, axis=-1, keepdims=True) + L2_EPS)


def _chunk_cumsum(g):
    pos = lax.broadcasted_iota(jnp.int32, g.shape, 0) % DELTA_CHUNK
    s = 1
    while s < DELTA_CHUNK:
        g = g + jnp.where(pos >= s, pltpu.roll(g, s, axis=0), 0.0)
        s *= 2
    return g


def _lane_column(x, lane_idx):
    lane = lax.broadcasted_iota(jnp.int32, x.shape, 1)
    col = jnp.sum(jnp.where(lane == lane_idx, x, 0.0), axis=1, keepdims=True)
    return jnp.broadcast_to(col, x.shape)


def _unit_lower_inverse(mats):
    c = mats[0].shape[0]
    row = lax.broadcasted_iota(jnp.int32, (c, c), 0)
    col = lax.broadcasted_iota(jnp.int32, (c, c), 1)
    apart = row ^ col
    eye = jnp.where(row == col, 1.0, 0.0)
    diag = [jnp.where(apart < INV_BASE, a, 0.0) for a in mats]
    ts = [eye - a for a in diag]
    apows = [a.astype(_bf16) for a in diag]
    apows = [_dot(ab, ab) for ab in apows]
    n = 2
    while n < INV_BASE:
        apbs = [ap.astype(_bf16) for ap in apows]
        if 2 * n < INV_BASE:
            both = [_dot(jnp.concatenate([t.astype(_bf16), apb], axis=0), apb)
                    for t, apb in zip(ts, apbs)]
            ts = [t + b[:c] for t, b in zip(ts, both)]
            apows = [b[c:] for b in both]
        else:
            ts = [t + _dot(t.astype(_bf16), apb) for t, apb in zip(ts, apbs)]
        n *= 2
    size = INV_BASE
    while size < c:
        lower_left = jnp.logical_and(apart >= size, apart < 2 * size)
        tbs = [t.astype(_bf16) for t in ts]
        xs = [_dot(jnp.where(lower_left, a, 0.0).astype(_bf16), tb) for a, tb in zip(mats, tbs)]
        ts = [t - _dot(tb, x.astype(_bf16)) for t, tb, x in zip(ts, tbs, xs)]
        size *= 2
    return ts


def _delta_chunks(probs, states):
    c = DELTA_CHUNK
    row = lax.broadcasted_iota(jnp.int32, (c, c), 0)
    col = lax.broadcasted_iota(jnp.int32, (c, c), 1)
    qn, kn, v, beta, gc = (list(t) for t in zip(*probs))
    n = len(probs)
    rng = range(n)
    decay = [jnp.exp(jnp.where(row >= col, gc[i] - gc[i].T, -jnp.inf)) for i in rng]
    qs = [qn[i] * (HEAD_DIM ** -0.5) for i in rng]
    kb = [kn[i].astype(_bf16) for i in rng]
    qk_kk = [_dot_nt(jnp.concatenate([qs[i].astype(_bf16), kb[i]], axis=0), kb[i]) for i in rng]
    qk = [(qk_kk[i][:c] * decay[i]).astype(_bf16) for i in rng]
    a_mat = [jnp.where(row > col, qk_kk[i][c:] * beta[i] * decay[i], 0.0) for i in rng]
    t_mat = _unit_lower_inverse(a_mat)
    eg = [jnp.exp(gc[i]) for i in rng]
    rhs = [jnp.concatenate([v[i] * beta[i], kn[i] * (beta[i] * eg[i])], axis=1).astype(_bf16)
           for i in rng]
    uw = [_dot(t_mat[i].astype(_bf16), rhs[i]) for i in rng]
    g_last = [gc[i][c - 1:c] for i in rng]
    k_dec_t = [(kn[i] * jnp.exp(g_last[i] - gc[i])).T.astype(_bf16) for i in rng]
    wq = [jnp.concatenate([uw[i][:, HEAD_DIM:], qs[i] * eg[i]], axis=0).astype(_bf16) for i in rng]
    ws_qs = [_dot(wq[i], states[i].astype(_bf16)) for i in rng]
    vb = [(uw[i][:, :HEAD_DIM] - ws_qs[i][:c]).astype(_bf16) for i in rng]
    outs = [ws_qs[i][c:] + _dot(qk[i], vb[i]) for i in rng]
    new_states = [states[i] * jnp.exp(g_last[i]) + _dot(k_dec_t[i], vb[i]) for i in rng]
    return outs, new_states


def _delta_kernel(q_ref, k_ref, v_ref, zs_ref, ba_ref, mq_ref, mk_ref, mv_ref, mba_ref,
                  cwq_ref, cwk_ref, cwv_ref, alog_ref, dtb_ref, onw_ref,
                  o_ref, state_ref, hq_ref, hk_ref, hv_ref):
    c = DELTA_CHUNK
    hd = HEAD_DIM
    heads = range(N_HEADS)

    def prep(q_raw, k_raw, v_raw, ba, hq, hk, hv, valid):
        qc = _conv_silu(q_raw, hq, cwq_ref[...])
        kc = _conv_silu(k_raw, hk, cwk_ref[...])
        vc = _conv_silu(v_raw, hv, cwv_ref[...])
        beta_all = _sigmoid(ba)
        g_all = -jnp.exp(alog_ref[...]) * _softplus(ba + dtb_ref[...])
        if valid is not None:
            beta_all = jnp.where(valid, beta_all, 0.0)
            g_all = jnp.where(valid, g_all, 0.0)
        gc_all = _chunk_cumsum(g_all)
        probs = []
        for h in heads:
            sl = slice(h * hd, (h + 1) * hd)
            probs.append((_l2_norm(qc[:, sl]), _l2_norm(kc[:, sl]), vc[:, sl],
                          _lane_column(beta_all, h), _lane_column(gc_all, N_HEADS + h)))
        return probs

    @pl.when(pl.program_id(1) == 0)
    def _():
        mq, mk, mv = mq_ref[...], mk_ref[...], mv_ref[...]
        zeros = jnp.zeros((HIST_ROWS, N_HEADS * hd), _f32)
        is_meta = lax.broadcasted_iota(jnp.int32, (c, 128), 0) >= c - N_META
        probs = prep(mq, mk, mv, mba_ref[...], zeros, zeros, zeros, is_meta)
        _, states = _delta_chunks(probs, [jnp.zeros((hd, hd), _f32) for _ in heads])
        for h in heads:
            state_ref[h] = states[h]
        hq_ref[...] = mq[c - HIST_ROWS:]
        hk_ref[...] = mk[c - HIST_ROWS:]
        hv_ref[...] = mv[c - HIST_ROWS:]

    q_raw, k_raw, v_raw = q_ref[...], k_ref[...], v_ref[...]
    probs = prep(q_raw, k_raw, v_raw, ba_ref[...], hq_ref[...], hk_ref[...], hv_ref[...], None)
    hq_ref[...] = q_raw[c - HIST_ROWS:]
    hk_ref[...] = k_raw[c - HIST_ROWS:]
    hv_ref[...] = v_raw[c - HIST_ROWS:]
    outs, states = _delta_chunks(probs, [state_ref[h] for h in heads])
    for h in heads:
        state_ref[h] = states[h]
        o = outs[h]
        o = o * lax.rsqrt(jnp.mean(o * o, axis=-1, keepdims=True) + RMS_EPS)
        sl = slice(h * hd, (h + 1) * hd)
        o_ref[:, sl] = (o * onw_ref[...] * zs_ref[:, sl].astype(_f32)).astype(o_ref.dtype)


def _delta_rule(lin, act, ba, meta_lin, meta_ba, conv_w, alog_row, dtb_row, onw_row, batch, zs_part):
    m = lin.shape[0]
    c = DELTA_CHUNK
    nblk = m // batch // c
    hd = HEAD_DIM
    w = N_HEADS * hd

    def tok(part):
        return pl.BlockSpec((c, w), lambda b, cb: (b * nblk + cb, part))

    def meta(part):
        return pl.BlockSpec((c, w), lambda b, cb: (0, part))

    def cw(part):
        return pl.BlockSpec((CONV_K, w), lambda b, cb: (0, part))

    row128 = pl.BlockSpec((1, 128), lambda b, cb: (0, 0))
    return pl.pallas_call(
        _delta_kernel,
        grid=(batch, nblk),
        in_specs=[
            tok(0), tok(1), tok(2), tok(zs_part),
            pl.BlockSpec((c, 128), lambda b, cb: (b * nblk + cb, 0)),
            meta(0), meta(1), meta(2),
            pl.BlockSpec((c, 128), lambda b, cb: (0, 0)),
            cw(0), cw(1), cw(2),
            row128, row128, row128,
        ],
        out_specs=pl.BlockSpec((c, w), lambda b, cb: (b * nblk + cb, 0)),
        out_shape=jax.ShapeDtypeStruct((m, w), _bf16),
        scratch_shapes=[
            pltpu.VMEM((N_HEADS, hd, hd), _f32),
            pltpu.VMEM((HIST_ROWS, w), _f32),
            pltpu.VMEM((HIST_ROWS, w), _f32),
            pltpu.VMEM((HIST_ROWS, w), _f32),
        ],
        compiler_params=pltpu.CompilerParams(
            dimension_semantics=("arbitrary", "arbitrary"),
            vmem_limit_bytes=_V7X_VMEM_LIMIT),
        name="delta_rule",
    )(lin, lin, lin, act, ba, meta_lin, meta_lin, meta_lin, meta_ba,
      conv_w, conv_w, conv_w, alog_row, dtb_row, onw_row)


def _merge_kernel(blocks_per_batch, p_ref, pprev_ref, pmeta_ref, sa_ref, sb_ref, o_ref, x_ref,
                  poolw_ref, pscale_ref, wpa_ref, wpb_ref, wout_ref, nw_ref, out_ref, hn_ref):
    first = (pl.program_id(0) % blocks_per_batch) == 0
    prev = jnp.where(first, pmeta_ref[...], pprev_ref[...])
    p = p_ref[...]
    ext = jnp.concatenate([prev, p], axis=0)
    gd = POOL_GROUP_DIM
    zb = []
    for gi, win in enumerate(POOL_WINDOWS):
        acc = ext[:, gi * gd:(gi + 1) * gd]
        s = 1
        while s < win:
            acc = acc + pltpu.roll(acc, s, axis=0)
            s *= 2
        pooled = acc[N_META:] * (1.0 / win) - p[:, gi * gd:(gi + 1) * gd]
        zb.append(_dot(pooled.astype(_bf16), poolw_ref[gi]))
    zb = (jnp.concatenate(zb, axis=1) * pscale_ref[...]).astype(_bf16)
    y_b = _dot(zb, wpb_ref[...])
    y_a = _dot(o_ref[...], wpa_ref[...])
    merged = sa_ref[...].astype(_f32) * y_a + sb_ref[...].astype(_f32) * y_b
    h1 = x_ref[...] + _dot(merged.astype(_bf16), wout_ref[...])
    out_ref[...] = h1
    hn_ref[...] = _rms_norm(h1, nw_ref[...]).astype(hn_ref.dtype)


def _merge(lin, act, meta_lin, o_gated, x, pool_w, pool_scale, w_proj_a, w_proj_b, w_out, ffn_norm_w,
           batch, tm, p_part):
    m, d = x.shape
    pd = pool_w.shape[0] * POOL_GROUP_DIM
    vd = o_gated.shape[1]
    blocks_per_batch = m // batch // tm
    sub = tm // N_META

    def const(shape):
        return pl.BlockSpec(shape, lambda i: (0,) * len(shape), pipeline_mode=pl.Buffered(1))

    return pl.pallas_call(
        functools.partial(_merge_kernel, blocks_per_batch),
        grid=(m // tm,),
        in_specs=[
            pl.BlockSpec((tm, pd), lambda i: (i, p_part)),
            pl.BlockSpec((N_META, pd), lambda i: (jnp.maximum(i * sub - 1, 0), p_part)),
            pl.BlockSpec((N_META, pd), lambda i: (0, p_part)),
            pl.BlockSpec((tm, d), lambda i: (i, 0)),
            pl.BlockSpec((tm, d), lambda i: (i, 1)),
            pl.BlockSpec((tm, vd), lambda i: (i, 0)),
            pl.BlockSpec((tm, d), lambda i: (i, 0)),
            const(pool_w.shape), const((1, pd)), const(w_proj_a.shape), const(w_proj_b.shape),
            const(w_out.shape), const((1, d)),
        ],
        out_specs=[pl.BlockSpec((tm, d), lambda i: (i, 0)), pl.BlockSpec((tm, d), lambda i: (i, 0))],
        out_shape=[jax.ShapeDtypeStruct((m, d), _f32), jax.ShapeDtypeStruct((m, d), _bf16)],
        compiler_params=pltpu.CompilerParams(
            dimension_semantics=("arbitrary",),
            vmem_limit_bytes=_V7X_VMEM_LIMIT),
        name="merge",
    )(lin, lin, meta_lin, act, act, o_gated, x, pool_w, pool_scale, w_proj_a, w_proj_b, w_out,
      ffn_norm_w)


def _ffn_kernel(h_hbm, hn_ref, wg_ref, wu_ref, wd_ref, fw_ref, out_ref, h_buf, h_sem):
    i = pl.program_id(0)
    f = pl.program_id(1)
    tm = out_ref.shape[0]

    def residual_copy():
        rows = pl.ds(pl.multiple_of(i * tm, tm), tm)
        return pltpu.make_async_copy(h_hbm.at[rows], h_buf, h_sem)

    @pl.when(f == 0)
    def _():
        residual_copy().start()
        out_ref[...] = jnp.zeros_like(out_ref)

    d = out_ref.shape[1]
    rc, dc = min(tm, 512), min(d, 512)
    for r0 in range(0, tm, rc):
        hn = hn_ref[r0:r0 + rc, :]
        act = (_silu(_dot(hn, wg_ref[...])) * _dot(hn, wu_ref[...])).astype(_bf16)
        for c0 in range(0, d, dc):
            out_ref[r0:r0 + rc, c0:c0 + dc] += _dot(act, wd_ref[:, c0:c0 + dc])

    @pl.when(f == pl.num_programs(1) - 1)
    def _():
        residual_copy().wait()
        out_ref[...] = _rms_norm(h_buf[...] + out_ref[...], fw_ref[...])


def _ffn(h, hn, w_gate, w_up, w_down, final_w, tm, tf):
    m, d = h.shape
    f = w_gate.shape[1]
    return pl.pallas_call(
        _ffn_kernel,
        grid=(m // tm, f // tf),
        in_specs=[
            pl.BlockSpec(memory_space=pl.ANY),
            pl.BlockSpec((tm, d), lambda i, j: (i, 0)),
            pl.BlockSpec((d, tf), lambda i, j: (0, j)),
            pl.BlockSpec((d, tf), lambda i, j: (0, j)),
            pl.BlockSpec((tf, d), lambda i, j: (j, 0)),
            pl.BlockSpec((1, d), lambda i, j: (0, 0)),
        ],
        out_specs=pl.BlockSpec((tm, d), lambda i, j: (i, 0)),
        out_shape=jax.ShapeDtypeStruct((m, d), _f32),
        scratch_shapes=[pltpu.VMEM((tm, d), _f32), pltpu.SemaphoreType.DMA(())],
        compiler_params=pltpu.CompilerParams(
            dimension_semantics=("arbitrary", "arbitrary"),
            vmem_limit_bytes=_V7X_VMEM_LIMIT),
        name="ffn",
    )(h, hn, w_gate, w_up, w_down, final_w)


def _largest_tile(n, cap):
    t = cap
    while n % t:
        t //= 2
    return t


def kernel(x, meta_tokens, norm_mix_w, w_in, conv_w, a_log, dt_bias, o_norm_w, w_proj_a, pool_w,
           pool_scale, w_proj_b, w_out, norm_ffn_w, w_ffn_gate, w_ffn_up, w_ffn_down, norm_final_w):
    assert w_in.shape[0] == 1, "single layer block"
    batch, seq, d = x.shape
    h, hd = N_HEADS, HEAD_DIM
    qk = h * hd
    pd = pool_w.shape[1] * POOL_GROUP_DIM
    assert seq % DELTA_CHUNK == 0 and N_META <= DELTA_CHUNK
    m = batch * seq

    w_t = w_in[0].T
    off_z = 3 * qk
    off_ba = 4 * qk
    off_p = off_ba + 2 * h
    off_g = off_p + pd
    tn = qk
    lin_cols, sig_cols = off_z + pd, 2 * d
    assert pd == qk and sig_cols % qk == 0
    row_offsets = ([0, qk, 2 * qk, off_p] + [off_g + c0 for c0 in range(0, sig_cols, tn)] + [off_z])
    w_main_t = _w_prep(w_t, row_offsets, tn, _largest_tile(d, 1024))
    w_ba_t = jnp.pad(w_t[off_ba:off_p], ((0, 128 - 2 * h), (0, 0))).astype(_bf16)
    p_part, zs_part = off_z // pd, sig_cols // qk

    x2 = x.reshape(m, d)
    nw = norm_mix_w[0].reshape(1, d)
    lin, act, ba = _in_proj(x2, nw, w_main_t, w_ba_t, _largest_tile(m, 1024), tn, lin_cols, sig_cols)
    meta_lin, _, meta_ba = _in_proj(meta_tokens, nw, w_main_t, w_ba_t, N_META, tn, lin_cols, sig_cols)
    pad = ((DELTA_CHUNK - N_META, 0), (0, 0))
    meta_lin_pad = jnp.pad(meta_lin, pad)
    meta_ba_pad = jnp.pad(meta_ba, pad)

    lane_pad = lambda a: jnp.pad(a.reshape(1, h), ((0, 0), (h, 128 - 2 * h)))
    o_gated = _delta_rule(lin, act, ba, meta_lin_pad, meta_ba_pad, conv_w[0], lane_pad(a_log[0]),
                          lane_pad(dt_bias[0]), o_norm_w[0].reshape(1, hd), batch, zs_part)

    h1, hn2 = _merge(lin, act, meta_lin, o_gated, x2, pool_w[0].astype(_bf16),
                     pool_scale[0].reshape(1, pd), w_proj_a[0].astype(_bf16),
                     w_proj_b[0].astype(_bf16), w_out[0].astype(_bf16), norm_ffn_w[0].reshape(1, d),
                     batch, _largest_tile(seq, 256), p_part)

    out = _ffn(h1, hn2, w_ffn_gate[0].astype(_bf16), w_ffn_up[0].astype(_bf16),
               w_ffn_down[0].astype(_bf16), norm_final_w.reshape(1, d), _largest_tile(m, 1024),
               _largest_tile(w_ffn_gate.shape[2], 512))
    return out.reshape(batch, seq, d)
```

```python
import functools

import jax
import jax.numpy as jnp
from jax import lax
from jax.experimental import pallas as pl
from jax.experimental.pallas import tpu as pltpu

N_META = 16
N_HEADS = 8
HEAD_DIM = 128
CONV_K = 4
POOL_WINDOWS = (2, 4, 8, 16)
POOL_GROUP_DIM = 256
RMS_EPS = 1e-6
L2_EPS = 1e-6

DELTA_CHUNK = 128
INV_BASE = 16
HIST_ROWS = 8

_V7X_VMEM_LIMIT = 56 * 1024 * 1024

_f32 = jnp.float32
_bf16 = jnp.bfloat16


def _dot(a, b):
    return jnp.dot(a, b, preferred_element_type=_f32)


def _dot_nt(a, b):
    return lax.dot_general(a, b, (((1,), (1,)), ((), ())), preferred_element_type=_f32)


def _sigmoid(x):
    return 1.0 / (1.0 + jnp.exp(-x))


def _silu(x):
    return x * _sigmoid(x)


def _softplus(x):
    return jnp.maximum(x, 0.0) + jnp.log1p(jnp.exp(-jnp.abs(x)))


def _rms_norm(x, w):
    return x * lax.rsqrt(jnp.mean(x * x, axis=-1, keepdims=True) + RMS_EPS) * w


def _w_prep_kernel(w_ref, o_ref):
    o_ref[...] = w_ref[...].astype(o_ref.dtype)


def _w_prep(w_t, row_offsets, tn, tk):
    n_src, k = w_t.shape
    assert all(off % 8 == 0 and off + tn <= n_src for off in row_offsets)

    def row_off(j):
        tile = 0
        for jj, o in enumerate(row_offsets):
            tile = jnp.where(j == jj, o // 8, tile)
        return tile * 8

    return pl.pallas_call(
        _w_prep_kernel,
        grid=(len(row_offsets), k // tk),
        in_specs=[pl.BlockSpec((pl.Element(tn), pl.Element(tk)),
                               lambda j, c: (row_off(j), c * tk))],
        out_specs=pl.BlockSpec((tn, tk), lambda j, c: (j, c)),
        out_shape=jax.ShapeDtypeStruct((len(row_offsets) * tn, k), _bf16),
        compiler_params=pltpu.CompilerParams(
            dimension_semantics=("arbitrary", "arbitrary"),
            vmem_limit_bytes=_V7X_VMEM_LIMIT),
        name="w_prep",
    )(w_t)


def _in_proj_kernel(n_lin, n_sig, x_ref, nw_ref, wba_ref, w_ref, lin_ref, act_ref, ba_ref, hn_ref):
    j = pl.program_id(1)

    @pl.when(j == 0)
    def _():
        hn = _rms_norm(x_ref[...], nw_ref[...]).astype(_bf16)
        hn_ref[...] = hn
        ba_ref[...] = _dot_nt(hn, wba_ref[...])

    @pl.when(j < n_lin)
    def _():
        lin_ref[...] = _dot_nt(hn_ref[...], w_ref[...])

    @pl.when(jnp.logical_and(j >= n_lin, j < n_lin + n_sig))
    def _():
        act_ref[...] = _sigmoid(_dot_nt(hn_ref[...], w_ref[...])).astype(act_ref.dtype)

    @pl.when(j >= n_lin + n_sig)
    def _():
        act_ref[...] = _silu(_dot_nt(hn_ref[...], w_ref[...])).astype(act_ref.dtype)


def _in_proj(x, norm_w, w_main_t, w_ba_t, tm, tn, lin_cols, sig_cols):
    m, d = x.shape
    n = w_main_t.shape[0]
    assert lin_cols % tn == 0 and sig_cols % tn == 0 and n % tn == 0
    n_lin, n_sig = lin_cols // tn, sig_cols // tn
    return pl.pallas_call(
        functools.partial(_in_proj_kernel, n_lin, n_sig),
        grid=(m // tm, n // tn),
        in_specs=[
            pl.BlockSpec((tm, d), lambda i, j: (i, 0)),
            pl.BlockSpec((1, d), lambda i, j: (0, 0)),
            pl.BlockSpec((128, d), lambda i, j: (0, 0)),
            pl.BlockSpec((tn, d), lambda i, j: (j, 0)),
        ],
        out_specs=[
            pl.BlockSpec((tm, tn), lambda i, j: (i, jnp.minimum(j, n_lin - 1))),
            pl.BlockSpec((tm, tn), lambda i, j: (i, jnp.maximum(j - n_lin, 0))),
            pl.BlockSpec((tm, 128), lambda i, j: (i, 0)),
        ],
        out_shape=[
            jax.ShapeDtypeStruct((m, lin_cols), _f32),
            jax.ShapeDtypeStruct((m, n - lin_cols), _bf16),
            jax.ShapeDtypeStruct((m, 128), _f32),
        ],
        scratch_shapes=[pltpu.VMEM((tm, d), _bf16)],
        compiler_params=pltpu.CompilerParams(
            dimension_semantics=("arbitrary", "arbitrary"),
            vmem_limit_bytes=_V7X_VMEM_LIMIT),
        name="in_proj",
    )(x, norm_w, w_ba_t, w_main_t)


def _conv_silu(x, hist, cw):
    assert CONV_K == 4
    ext = jnp.concatenate([hist, x], axis=0)
    ext1 = pltpu.roll(ext, 1, axis=0)
    old = ext * cw[1:2] + ext1 * cw[0:1]
    acc = ext * cw[3:4] + ext1 * cw[2:3] + pltpu.roll(old, 2, axis=0)
    return _silu(acc[HIST_ROWS:])


def _l2_norm(x):
    return x * lax.rsqrt(jnp.sum(x * x, axis=-1, keepdims=True) + L2_EPS)


def _chunk_cumsum(g):
    pos = lax.broadcasted_iota(jnp.int32, g.shape, 0) % DELTA_CHUNK
    s = 1
    while s < DELTA_CHUNK:
        g = g + jnp.where(pos >= s, pltpu.roll(g, s, axis=0), 0.0)
        s *= 2
    return g


def _lane_column(x, lane_idx):
    lane = lax.broadcasted_iota(jnp.int32, x.shape, 1)
    col = jnp.sum(jnp.where(lane == lane_idx, x, 0.0), axis=1, keepdims=True)
    return jnp.broadcast_to(col, x.shape)


LEVEL_GROUP = 4
MERGE_UNIT_COLS = 256


def _level(thunks):
    out = []
    for i, thunk in enumerate(thunks):
        out.append(thunk())
        if i % LEVEL_GROUP == LEVEL_GROUP - 1 or i == len(thunks) - 1:
            yield
    return out


def _unit_lower_inverse(mats):
    c = mats[0].shape[0]
    row = lax.broadcasted_iota(jnp.int32, (c, c), 0)
    col = lax.broadcasted_iota(jnp.int32, (c, c), 1)
    apart = row ^ col
    eye = jnp.where(row == col, 1.0, 0.0)
    diag = [jnp.where(apart < INV_BASE, a, 0.0) for a in mats]
    ts = [eye - a for a in diag]
    apows = yield from _level([lambda a=a: _dot(a.astype(_bf16), a.astype(_bf16)) for a in diag])
    n = 2
    while n < INV_BASE:
        apbs = [ap.astype(_bf16) for ap in apows]
        if 2 * n < INV_BASE:
            both = yield from _level(
                [lambda t=t, apb=apb: _dot(jnp.concatenate([t.astype(_bf16), apb], axis=0), apb)
                 for t, apb in zip(ts, apbs)])
            ts = [t + b[:c] for t, b in zip(ts, both)]
            apows = [b[c:] for b in both]
        else:
            ts = yield from _level([lambda t=t, apb=apb: t + _dot(t.astype(_bf16), apb)
                                    for t, apb in zip(ts, apbs)])
        n *= 2
    size = INV_BASE
    while size < c:
        lower_left = jnp.logical_and(apart >= size, apart < 2 * size)
        tbs = [t.astype(_bf16) for t in ts]
        xs = yield from _level([lambda a=a, tb=tb: _dot(jnp.where(lower_left, a, 0.0).astype(_bf16), tb)
                                for a, tb in zip(mats, tbs)])
        ts = yield from _level([lambda t=t, tb=tb, x=x: t - _dot(tb, x.astype(_bf16))
                                for t, tb, x in zip(ts, tbs, xs)])
        size *= 2
    return ts


def _delta_chunks(probs, states):
    c = DELTA_CHUNK
    row = lax.broadcasted_iota(jnp.int32, (c, c), 0)
    col = lax.broadcasted_iota(jnp.int32, (c, c), 1)
    qn, kn, v, beta, gc = (list(t) for t in zip(*probs))
    n = len(probs)
    rng = range(n)
    decay = [jnp.exp(jnp.where(row >= col, gc[i] - gc[i].T, -jnp.inf)) for i in rng]
    qs = [qn[i] * (HEAD_DIM ** -0.5) for i in rng]
    kb = [kn[i].astype(_bf16) for i in rng]
    qk_kk = yield from _level(
        [lambda i=i: _dot_nt(jnp.concatenate([qs[i].astype(_bf16), kb[i]], axis=0), kb[i]) for i in rng])
    qk = [(qk_kk[i][:c] * decay[i]).astype(_bf16) for i in rng]
    a_mat = [jnp.where(row > col, qk_kk[i][c:] * beta[i] * decay[i], 0.0) for i in rng]
    t_mat = yield from _unit_lower_inverse(a_mat)
    eg = [jnp.exp(gc[i]) for i in rng]
    rhs = [jnp.concatenate([v[i] * beta[i], kn[i] * (beta[i] * eg[i])], axis=1).astype(_bf16)
           for i in rng]
    uw = yield from _level([lambda i=i: _dot(t_mat[i].astype(_bf16), rhs[i]) for i in rng])
    g_last = [gc[i][c - 1:c] for i in rng]
    k_dec_t = [(kn[i] * jnp.exp(g_last[i] - gc[i])).T.astype(_bf16) for i in rng]
    wq = [jnp.concatenate([uw[i][:, HEAD_DIM:], qs[i] * eg[i]], axis=0).astype(_bf16) for i in rng]
    ws_qs = yield from _level([lambda i=i: _dot(wq[i], states[i].astype(_bf16)) for i in rng])
    vb = [(uw[i][:, :HEAD_DIM] - ws_qs[i][:c]).astype(_bf16) for i in rng]
    outs = yield from _level([lambda i=i: ws_qs[i][c:] + _dot(qk[i], vb[i]) for i in rng])
    new_states = yield from _level(
        [lambda i=i: states[i] * jnp.exp(g_last[i]) + _dot(k_dec_t[i], vb[i]) for i in rng])
    return outs, new_states


def _run_interleaved(gen, fillers=(), plan=()):
    fillers = list(fillers)
    for count in list(plan) + [0] * 64:
        try:
            next(gen)
        except StopIteration as stop:
            for f in fillers:
                f()
            return stop.value
        for _ in range(min(count, len(fillers))):
            fillers.pop(0)()
    raise AssertionError("generator yielded more often than planned for")


def _merge_pieces(first, p_ref, pprev_ref, pmeta_ref, sa_ref, sb_ref, o_gated, x_ref,
                  poolw_ref, pscale_ref, wpa_ref, wpb_ref, wout_ref, nw_ref, out_ref, hn_ref):
    d = x_ref.shape[1]
    unit = MERGE_UNIT_COLS
    cols = range(0, d, unit)
    val = {}

    def pool():
        prev = jnp.where(first, pmeta_ref[...], pprev_ref[...])
        p = p_ref[...]
        ext = jnp.concatenate([prev, p], axis=0)
        gd = POOL_GROUP_DIM
        zb = []
        for gi, win in enumerate(POOL_WINDOWS):
            acc = ext[:, gi * gd:(gi + 1) * gd]
            s = 1
            while s < win:
                acc = acc + pltpu.roll(acc, s, axis=0)
                s *= 2
            pooled = acc[N_META:] * (1.0 / win) - p[:, gi * gd:(gi + 1) * gd]
            zb.append(_dot(pooled.astype(_bf16), poolw_ref[gi]))
        val["zb"] = (jnp.concatenate(zb, axis=1) * pscale_ref[...]).astype(_bf16)

    def branch(name, lhs, w_ref, gate_ref, c0):
        def run():
            y = _dot(lhs(), w_ref[:, c0:c0 + unit])
            val[name, c0] = gate_ref[:, c0:c0 + unit].astype(_f32) * y
        return run

    def out_cols(c0):
        def run():
            if "merged" not in val:
                merged = [val["a", m0] + val["b", m0] for m0 in cols]
                val["merged"] = jnp.concatenate(merged, axis=1).astype(_bf16)
            val["h", c0] = x_ref[:, c0:c0 + unit] + _dot(val["merged"], wout_ref[:, c0:c0 + unit])
        return run

    def finish():
        h1 = jnp.concatenate([val["h", c0] for c0 in cols], axis=1)
        out_ref[...] = h1
        hn_ref[...] = _rms_norm(h1, nw_ref[...]).astype(hn_ref.dtype)

    branches = ([branch("b", lambda: val["zb"], wpb_ref, sb_ref, c0) for c0 in cols]
                + [branch("a", o_gated, wpa_ref, sa_ref, c0) for c0 in cols])
    return [pool] + branches, [out_cols(c0) for c0 in cols] + [finish]


def _mixer_kernel(nblk, q_ref, k_ref, v_ref, zs_ref, ba_ref, mq_ref, mk_ref, mv_ref, mba_ref,
                  cwq_ref, cwk_ref, cwv_ref, alog_ref, dtb_ref, onw_ref,
                  p_ref, pprev_ref, pmeta_ref, sa_ref, sb_ref, x_ref,
                  poolw_ref, pscale_ref, wpa_ref, wpb_ref, wout_ref, nw_ref,
                  out_ref, hn_ref, state_ref, hq_ref, hk_ref, hv_ref, obuf_ref):
    c = DELTA_CHUNK
    hd = HEAD_DIM
    heads = range(N_HEADS)
    s = pl.program_id(1)

    def delta_step(q_raw, k_raw, v_raw, ba, hq, hk, hv, valid, states):
        qc = _conv_silu(q_raw, hq, cwq_ref[...])
        yield
        kc = _conv_silu(k_raw, hk, cwk_ref[...])
        yield
        vc = _conv_silu(v_raw, hv, cwv_ref[...])
        yield
        beta_all = _sigmoid(ba)
        g_all = -jnp.exp(alog_ref[...]) * _softplus(ba + dtb_ref[...])
        if valid is not None:
            beta_all = jnp.where(valid, beta_all, 0.0)
            g_all = jnp.where(valid, g_all, 0.0)
        gc_all = _chunk_cumsum(g_all)
        probs = []
        for h in heads:
            sl = slice(h * hd, (h + 1) * hd)
            probs.append((_l2_norm(qc[:, sl]), _l2_norm(kc[:, sl]), vc[:, sl],
                          _lane_column(beta_all, h), _lane_column(gc_all, N_HEADS + h)))
        yield
        return (yield from _delta_chunks(probs, states))

    @pl.when(s == 0)
    def _():
        mq, mk, mv = mq_ref[...], mk_ref[...], mv_ref[...]
        zeros = jnp.zeros((HIST_ROWS, N_HEADS * hd), _f32)
        is_meta = lax.broadcasted_iota(jnp.int32, (c, 128), 0) >= c - N_META
        _, states = _run_interleaved(delta_step(
            mq, mk, mv, mba_ref[...], zeros, zeros, zeros, is_meta,
            [jnp.zeros((hd, hd), _f32) for _ in heads]))
        for h in heads:
            state_ref[h] = states[h]
        hq_ref[...] = mq[c - HIST_ROWS:]
        hk_ref[...] = mk[c - HIST_ROWS:]
        hv_ref[...] = mv[c - HIST_ROWS:]
        obuf_ref[...] = jnp.zeros_like(obuf_ref)

    q_raw, k_raw, v_raw = q_ref[...], k_ref[...], v_ref[...]
    slot = s % 2
    early, late = _merge_pieces(s <= 1, p_ref, pprev_ref, pmeta_ref, sa_ref, sb_ref,
                                lambda: obuf_ref[1 - slot],
                                x_ref, poolw_ref, pscale_ref, wpa_ref, wpb_ref, wout_ref, nw_ref,
                                out_ref, hn_ref)
    base, extra = divmod(len(early), 4)
    outs, states = _run_interleaved(
        delta_step(q_raw, k_raw, v_raw, ba_ref[...], hq_ref[...], hk_ref[...], hv_ref[...], None,
                   [state_ref[h] for h in heads]),
        early + late, plan=[base + 1] * extra + [base] * (4 - extra) + [0, 0, 1] * len(late))
    hq_ref[...] = q_raw[c - HIST_ROWS:]
    hk_ref[...] = k_raw[c - HIST_ROWS:]
    hv_ref[...] = v_raw[c - HIST_ROWS:]
    gated = []
    for h in heads:
        state_ref[h] = states[h]
        o = outs[h]
        o = o * lax.rsqrt(jnp.mean(o * o, axis=-1, keepdims=True) + RMS_EPS)
        sl = slice(h * hd, (h + 1) * hd)
        gated.append((o * onw_ref[...] * zs_ref[:, sl].astype(_f32)).astype(obuf_ref.dtype))
    obuf_ref[slot] = jnp.concatenate(gated, axis=1)


def _mixer(lin, act, ba, meta_lin, meta_lin_pad, meta_ba_pad, conv_w, alog_row, dtb_row, onw_row, x,
           pool_w, pool_scale, w_proj_a, w_proj_b, w_out, ffn_norm_w, batch, p_part, zs_part):
    m, d = x.shape
    c = DELTA_CHUNK
    nblk = m // batch // c
    hd = HEAD_DIM
    w = N_HEADS * hd
    pd = pool_w.shape[0] * POOL_GROUP_DIM
    assert pd == w
    sub = c // N_META

    def delta_row(b, s):
        return b * nblk + jnp.minimum(s, nblk - 1)

    def merge_row(b, s):
        return b * nblk + jnp.maximum(s - 1, 0)

    def tok(part):
        return pl.BlockSpec((c, w), lambda b, s: (delta_row(b, s), part))

    def meta(part):
        return pl.BlockSpec((c, w), lambda b, s: (0, part))

    def cw(part):
        return pl.BlockSpec((CONV_K, w), lambda b, s: (0, part))

    def const(shape):
        return pl.BlockSpec(shape, lambda b, s: (0,) * len(shape), pipeline_mode=pl.Buffered(1))

    row128 = pl.BlockSpec((1, 128), lambda b, s: (0, 0))
    return pl.pallas_call(
        functools.partial(_mixer_kernel, nblk),
        grid=(batch, nblk + 1),
        in_specs=[
            tok(0), tok(1), tok(2), tok(zs_part),
            pl.BlockSpec((c, 128), lambda b, s: (delta_row(b, s), 0)),
            meta(0), meta(1), meta(2),
            pl.BlockSpec((c, 128), lambda b, s: (0, 0)),
            cw(0), cw(1), cw(2),
            row128, row128, row128,
            pl.BlockSpec((c, pd), lambda b, s: (merge_row(b, s), p_part)),
            pl.BlockSpec((N_META, pd), lambda b, s: (jnp.maximum(merge_row(b, s) * sub - 1, 0), p_part)),
            pl.BlockSpec((N_META, pd), lambda b, s: (0, p_part)),
            pl.BlockSpec((c, d), lambda b, s: (merge_row(b, s), 0)),
            pl.BlockSpec((c, d), lambda b, s: (merge_row(b, s), 1)),
            pl.BlockSpec((c, d), lambda b, s: (merge_row(b, s), 0)),
            const(pool_w.shape), const((1, pd)), const(w_proj_a.shape), const(w_proj_b.shape),
            const(w_out.shape), const((1, d)),
        ],
        out_specs=[pl.BlockSpec((c, d), lambda b, s: (merge_row(b, s), 0)),
                   pl.BlockSpec((c, d), lambda b, s: (merge_row(b, s), 0))],
        out_shape=[jax.ShapeDtypeStruct((m, d), _f32), jax.ShapeDtypeStruct((m, d), _bf16)],
        scratch_shapes=[
            pltpu.VMEM((N_HEADS, hd, hd), _f32),
            pltpu.VMEM((HIST_ROWS, w), _f32),
            pltpu.VMEM((HIST_ROWS, w), _f32),
            pltpu.VMEM((HIST_ROWS, w), _f32),
            pltpu.VMEM((2, c, w), _bf16),
        ],
        compiler_params=pltpu.CompilerParams(
            dimension_semantics=("arbitrary", "arbitrary"),
            vmem_limit_bytes=_V7X_VMEM_LIMIT),
        name="mixer",
    )(lin, lin, lin, act, ba, meta_lin_pad, meta_lin_pad, meta_lin_pad, meta_ba_pad,
      conv_w, conv_w, conv_w, alog_row, dtb_row, onw_row,
      lin, lin, meta_lin, act, act, x,
      pool_w, pool_scale, w_proj_a, w_proj_b, w_out, ffn_norm_w)


def _ffn_kernel(h_hbm, hn_ref, wg_ref, wu_ref, wd_ref, fw_ref, out_ref, h_buf, h_sem):
    i = pl.program_id(0)
    f = pl.program_id(1)
    tm = out_ref.shape[0]

    def residual_copy():
        rows = pl.ds(pl.multiple_of(i * tm, tm), tm)
        return pltpu.make_async_copy(h_hbm.at[rows], h_buf, h_sem)

    @pl.when(f == 0)
    def _():
        residual_copy().start()
        out_ref[...] = jnp.zeros_like(out_ref)

    d = out_ref.shape[1]
    rc, dc = min(tm, 512), min(d, 512)
    for r0 in range(0, tm, rc):
        hn = hn_ref[r0:r0 + rc, :]
        act = (_silu(_dot(hn, wg_ref[...])) * _dot(hn, wu_ref[...])).astype(_bf16)
        for c0 in range(0, d, dc):
            out_ref[r0:r0 + rc, c0:c0 + dc] += _dot(act, wd_ref[:, c0:c0 + dc])

    @pl.when(f == pl.num_programs(1) - 1)
    def _():
        residual_copy().wait()
        out_ref[...] = _rms_norm(h_buf[...] + out_ref[...], fw_ref[...])


def _ffn(h, hn, w_gate, w_up, w_down, final_w, tm, tf):
    m, d = h.shape
    f = w_gate.shape[1]
    return pl.pallas_call(
        _ffn_kernel,
        grid=(m // tm, f // tf),
        in_specs=[
            pl.BlockSpec(memory_space=pl.ANY),
            pl.BlockSpec((tm, d), lambda i, j: (i, 0)),
            pl.BlockSpec((d, tf), lambda i, j: (0, j)),
            pl.BlockSpec((d, tf), lambda i, j: (0, j)),
            pl.BlockSpec((tf, d), lambda i, j: (j, 0)),
            pl.BlockSpec((1, d), lambda i, j: (0, 0)),
        ],
        out_specs=pl.BlockSpec((tm, d), lambda i, j: (i, 0)),
        out_shape=jax.ShapeDtypeStruct((m, d), _f32),
        scratch_shapes=[pltpu.VMEM((tm, d), _f32), pltpu.SemaphoreType.DMA(())],
        compiler_params=pltpu.CompilerParams(
            dimension_semantics=("arbitrary", "arbitrary"),
            vmem_limit_bytes=_V7X_VMEM_LIMIT),
        name="ffn",
    )(h, hn, w_gate, w_up, w_down, final_w)


def _largest_tile(n, cap):
    t = cap
    while n % t:
        t //= 2
    return t


def kernel(x, meta_tokens, norm_mix_w, w_in, conv_w, a_log, dt_bias, o_norm_w, w_proj_a, pool_w,
           pool_scale, w_proj_b, w_out, norm_ffn_w, w_ffn_gate, w_ffn_up, w_ffn_down, norm_final_w):
    assert w_in.shape[0] == 1, "single layer block"
    batch, seq, d = x.shape
    h, hd = N_HEADS, HEAD_DIM
    qk = h * hd
    pd = pool_w.shape[1] * POOL_GROUP_DIM
    assert seq % DELTA_CHUNK == 0 and N_META <= DELTA_CHUNK
    m = batch * seq

    w_t = w_in[0].T
    off_z = 3 * qk
    off_ba = 4 * qk
    off_p = off_ba + 2 * h
    off_g = off_p + pd
    tn = qk
    lin_cols, sig_cols = off_z + pd, 2 * d
    assert pd == qk and sig_cols % qk == 0
    row_offsets = ([0, qk, 2 * qk, off_p] + [off_g + c0 for c0 in range(0, sig_cols, tn)] + [off_z])
    w_main_t = _w_prep(w_t, row_offsets, tn, _largest_tile(d, 1024))
    w_ba_t = jnp.pad(w_t[off_ba:off_p], ((0, 128 - 2 * h), (0, 0))).astype(_bf16)
    p_part, zs_part = off_z // pd, sig_cols // qk

    x2 = x.reshape(m, d)
    nw = norm_mix_w[0].reshape(1, d)
    lin, act, ba = _in_proj(x2, nw, w_main_t, w_ba_t, _largest_tile(m, 1024), tn, lin_cols, sig_cols)
    meta_lin, _, meta_ba = _in_proj(meta_tokens, nw, w_main_t, w_ba_t, N_META, tn, lin_cols, sig_cols)
    pad = ((DELTA_CHUNK - N_META, 0), (0, 0))
    meta_lin_pad = jnp.pad(meta_lin, pad)
    meta_ba_pad = jnp.pad(meta_ba, pad)

    lane_pad = lambda a: jnp.pad(a.reshape(1, h), ((0, 0), (h, 128 - 2 * h)))
    h1, hn2 = _mixer(lin, act, ba, meta_lin, meta_lin_pad, meta_ba_pad, conv_w[0], lane_pad(a_log[0]),
                     lane_pad(dt_bias[0]), o_norm_w[0].reshape(1, hd), x2, pool_w[0].astype(_bf16),
                     pool_scale[0].reshape(1, pd), w_proj_a[0].astype(_bf16),
                     w_proj_b[0].astype(_bf16), w_out[0].astype(_bf16), norm_ffn_w[0].reshape(1, d),
                     batch, p_part, zs_part)

    out = _ffn(h1, hn2, w_ffn_gate[0].astype(_bf16), w_ffn_up[0].astype(_bf16),
               w_ffn_down[0].astype(_bf16), norm_final_w.reshape(1, d), _largest_tile(m, 1024),
               _largest_tile(w_ffn_gate.shape[2], 512))
    return out.reshape(batch, seq, d)
```

```python
import functools

import jax
import jax.numpy as jnp
from jax import lax
from jax.experimental import pallas as pl
from jax.experimental.pallas import tpu as pltpu

N_META = 16
N_HEADS = 8
HEAD_DIM = 128
CONV_K = 4
POOL_WINDOWS = (2, 4, 8, 16)
POOL_GROUP_DIM = 256
RMS_EPS = 1e-6
L2_EPS = 1e-6

DELTA_CHUNK = 128
INV_BASE = 16
HIST_ROWS = 8

_V7X_VMEM_LIMIT = 56 * 1024 * 1024

_f32 = jnp.float32
_bf16 = jnp.bfloat16


def _dot(a, b):
    return jnp.dot(a, b, preferred_element_type=_f32)


def _dot_nt(a, b):
    return lax.dot_general(a, b, (((1,), (1,)), ((), ())), preferred_element_type=_f32)


def _sigmoid(x):
    return 1.0 / (1.0 + jnp.exp(-x))


def _silu(x):
    return x * _sigmoid(x)


def _softplus(x):
    return jnp.maximum(x, 0.0) + jnp.log1p(jnp.exp(-jnp.abs(x)))


def _rms_norm(x, w):
    return x * lax.rsqrt(jnp.mean(x * x, axis=-1, keepdims=True) + RMS_EPS) * w


def _w_prep_kernel(w_ref, o_ref):
    o_ref[...] = w_ref[...].astype(o_ref.dtype)


def _w_prep(w_t, row_offsets, tn, tk):
    n_src, k = w_t.shape
    assert all(off % 8 == 0 and off + tn <= n_src for off in row_offsets)

    def row_off(j):
        tile = 0
        for jj, o in enumerate(row_offsets):
            tile = jnp.where(j == jj, o // 8, tile)
        return tile * 8

    return pl.pallas_call(
        _w_prep_kernel,
        grid=(len(row_offsets), k // tk),
        in_specs=[pl.BlockSpec((pl.Element(tn), pl.Element(tk)),
                               lambda j, c: (row_off(j), c * tk))],
        out_specs=pl.BlockSpec((tn, tk), lambda j, c: (j, c)),
        out_shape=jax.ShapeDtypeStruct((len(row_offsets) * tn, k), _bf16),
        compiler_params=pltpu.CompilerParams(
            dimension_semantics=("arbitrary", "arbitrary"),
            vmem_limit_bytes=_V7X_VMEM_LIMIT),
        name="w_prep",
    )(w_t)


CAST_COL_SPLIT = 4


def _sigmoid_tanh(x):
    return 0.5 * jnp.tanh(0.5 * x) + 0.5


def _in_proj_kernel(n_lin, n_sig, cast_starts, x_ref, nw_ref, wba_ref, w_ref, *refs):
    n_cast = len(cast_starts)
    cast_src = refs[:n_cast]
    lin_ref, act_ref, ba_ref = refs[n_cast:n_cast + 3]
    cast_dst = refs[n_cast + 3:2 * n_cast + 3]
    hn_ref = refs[2 * n_cast + 3]
    j = pl.program_id(1)

    @pl.when(j == 0)
    def _():
        hn = _rms_norm(x_ref[...], nw_ref[...]).astype(_bf16)
        hn_ref[...] = hn
        ba_ref[...] = _dot_nt(hn, wba_ref[...])

    @pl.when(j < n_lin)
    def _():
        lin_ref[...] = _dot_nt(hn_ref[...], w_ref[...])

    @pl.when(jnp.logical_and(j >= n_lin, j < n_lin + n_sig))
    def _():
        act_ref[...] = _sigmoid_tanh(_dot_nt(hn_ref[...], w_ref[...])).astype(act_ref.dtype)

    @pl.when(j >= n_lin + n_sig)
    def _():
        y = _dot_nt(hn_ref[...], w_ref[...])
        act_ref[...] = (y * _sigmoid_tanh(y)).astype(act_ref.dtype)

    for start, src, dst in zip(cast_starts, cast_src, cast_dst):
        @pl.when(jnp.logical_and(j >= start, j < start + CAST_COL_SPLIT))
        def _(src=src, dst=dst):
            dst[...] = src[...].astype(dst.dtype)


def _in_proj(x, norm_w, w_main_t, w_ba_t, tm, tn, lin_cols, sig_cols, casts=()):
    m, d = x.shape
    n = w_main_t.shape[0]
    assert lin_cols % tn == 0 and sig_cols % tn == 0 and n % tn == 0
    n_lin, n_sig = lin_cols // tn, sig_cols // tn
    gi, gj = m // tm, n // tn
    assert gj >= CAST_COL_SPLIT
    cast_starts = tuple(min(CAST_COL_SPLIT * k, gj - CAST_COL_SPLIT) for k in range(len(casts)))
    cast_specs = []
    for start, a in zip(cast_starts, casts):
        rows, cols = a.shape
        assert rows % (16 * gi) == 0 and cols % (128 * CAST_COL_SPLIT) == 0
        cast_specs.append(pl.BlockSpec(
            (rows // gi, cols // CAST_COL_SPLIT),
            lambda i, j, start=start: (i, jnp.clip(j - start, 0, CAST_COL_SPLIT - 1))))
    return pl.pallas_call(
        functools.partial(_in_proj_kernel, n_lin, n_sig, cast_starts),
        grid=(gi, gj),
        in_specs=[
            pl.BlockSpec((tm, d), lambda i, j: (i, 0)),
            pl.BlockSpec((1, d), lambda i, j: (0, 0)),
            pl.BlockSpec((128, d), lambda i, j: (0, 0)),
            pl.BlockSpec((tn, d), lambda i, j: (j, 0)),
        ] + cast_specs,
        out_specs=[
            pl.BlockSpec((tm, tn), lambda i, j: (i, jnp.minimum(j, n_lin - 1))),
            pl.BlockSpec((tm, tn), lambda i, j: (i, jnp.maximum(j - n_lin, 0))),
            pl.BlockSpec((tm, 128), lambda i, j: (i, 0)),
        ] + cast_specs,
        out_shape=[
            jax.ShapeDtypeStruct((m, lin_cols), _f32),
            jax.ShapeDtypeStruct((m, n - lin_cols), _bf16),
            jax.ShapeDtypeStruct((m, 128), _f32),
        ] + [jax.ShapeDtypeStruct(a.shape, _bf16) for a in casts],
        scratch_shapes=[pltpu.VMEM((tm, d), _bf16)],
        compiler_params=pltpu.CompilerParams(
            dimension_semantics=("arbitrary", "arbitrary"),
            vmem_limit_bytes=_V7X_VMEM_LIMIT),
        name="in_proj",
    )(x, norm_w, w_ba_t, w_main_t, *casts)


def _conv_silu(x, hist, cw):
    assert CONV_K == 4
    ext = jnp.concatenate([hist, x], axis=0)
    ext1 = pltpu.roll(ext, 1, axis=0)
    old = ext * cw[1:2] + ext1 * cw[0:1]
    acc = ext * cw[3:4] + ext1 * cw[2:3] + pltpu.roll(old, 2, axis=0)
    return _silu(acc[HIST_ROWS:])


def _l2_norm(x):
    return x * lax.rsqrt(jnp.sum(x * x, axis=-1, keepdims=True) + L2_EPS)


def _chunk_cumsum(g):
    pos = lax.broadcasted_iota(jnp.int32, g.shape, 0) % DELTA_CHUNK
    s = 1
    while s < DELTA_CHUNK:
        g = g + jnp.where(pos >= s, pltpu.roll(g, s, axis=0), 0.0)
        s *= 2
    return g


def _lane_column(x, lane_idx):
    lane = lax.broadcasted_iota(jnp.int32, x.shape, 1)
    col = jnp.sum(jnp.where(lane == lane_idx, x, 0.0), axis=1, keepdims=True)
    return jnp.broadcast_to(col, x.shape)


LEVEL_GROUP = 4
MERGE_UNIT_COLS = 256


def _level(thunks):
    out = []
    for i, thunk in enumerate(thunks):
        out.append(thunk())
        if i % LEVEL_GROUP == LEVEL_GROUP - 1 or i == len(thunks) - 1:
            yield
    return out


def _unit_lower_inverse(mats):
    c = mats[0].shape[0]
    row = lax.broadcasted_iota(jnp.int32, (c, c), 0)
    col = lax.broadcasted_iota(jnp.int32, (c, c), 1)
    apart = row ^ col
    eye = jnp.where(row == col, 1.0, 0.0)
    diag = [jnp.where(apart < INV_BASE, a, 0.0) for a in mats]
    ts = [eye - a for a in diag]
    apows = yield from _level([lambda a=a: _dot(a.astype(_bf16), a.astype(_bf16)) for a in diag])
    n = 2
    while n < INV_BASE:
        apbs = [ap.astype(_bf16) for ap in apows]
        if 2 * n < INV_BASE:
            both = yield from _level(
                [lambda t=t, apb=apb: _dot(jnp.concatenate([t.astype(_bf16), apb], axis=0), apb)
                 for t, apb in zip(ts, apbs)])
            ts = [t + b[:c] for t, b in zip(ts, both)]
            apows = [b[c:] for b in both]
        else:
            ts = yield from _level([lambda t=t, apb=apb: t + _dot(t.astype(_bf16), apb)
                                    for t, apb in zip(ts, apbs)])
        n *= 2
    size = INV_BASE
    while size < c:
        lower_left = jnp.logical_and(apart >= size, apart < 2 * size)
        tbs = [t.astype(_bf16) for t in ts]
        xs = yield from _level([lambda a=a, tb=tb: _dot(jnp.where(lower_left, a, 0.0).astype(_bf16), tb)
                                for a, tb in zip(mats, tbs)])
        ts = yield from _level([lambda t=t, tb=tb, x=x: t - _dot(tb, x.astype(_bf16))
                                for t, tb, x in zip(ts, tbs, xs)])
        size *= 2
    return ts


def _delta_chunks(probs, states):
    c = DELTA_CHUNK
    row = lax.broadcasted_iota(jnp.int32, (c, c), 0)
    col = lax.broadcasted_iota(jnp.int32, (c, c), 1)
    qn, kn, v, beta, gc = (list(t) for t in zip(*probs))
    n = len(probs)
    rng = range(n)
    decay = [jnp.exp(jnp.where(row >= col, gc[i] - gc[i].T, -jnp.inf)) for i in rng]
    qs = [qn[i] * (HEAD_DIM ** -0.5) for i in rng]
    kb = [kn[i].astype(_bf16) for i in rng]
    qk_kk = yield from _level(
        [lambda i=i: _dot_nt(jnp.concatenate([qs[i].astype(_bf16), kb[i]], axis=0), kb[i]) for i in rng])
    qk = [(qk_kk[i][:c] * decay[i]).astype(_bf16) for i in rng]
    a_mat = [jnp.where(row > col, qk_kk[i][c:] * beta[i] * decay[i], 0.0) for i in rng]
    t_mat = yield from _unit_lower_inverse(a_mat)
    eg = [jnp.exp(gc[i]) for i in rng]
    rhs = [jnp.concatenate([v[i] * beta[i], kn[i] * (beta[i] * eg[i])], axis=1).astype(_bf16)
           for i in rng]
    uw = yield from _level([lambda i=i: _dot(t_mat[i].astype(_bf16), rhs[i]) for i in rng])
    g_last = [gc[i][c - 1:c] for i in rng]
    k_dec_t = [(kn[i] * jnp.exp(g_last[i] - gc[i])).T.astype(_bf16) for i in rng]
    wq = [jnp.concatenate([uw[i][:, HEAD_DIM:], qs[i] * eg[i]], axis=0).astype(_bf16) for i in rng]
    ws_qs = yield from _level([lambda i=i: _dot(wq[i], states[i].astype(_bf16)) for i in rng])
    vb = [(uw[i][:, :HEAD_DIM] - ws_qs[i][:c]).astype(_bf16) for i in rng]
    outs = yield from _level([lambda i=i: ws_qs[i][c:] + _dot(qk[i], vb[i]) for i in rng])
    new_states = yield from _level(
        [lambda i=i: states[i] * jnp.exp(g_last[i]) + _dot(k_dec_t[i], vb[i]) for i in rng])
    return outs, new_states


def _run_interleaved(gen, fillers=(), plan=()):
    fillers = list(fillers)
    for count in list(plan) + [0] * 64:
        try:
            next(gen)
        except StopIteration as stop:
            for f in fillers:
                f()
            return stop.value
        for _ in range(min(count, len(fillers))):
            fillers.pop(0)()
    raise AssertionError("generator yielded more often than planned for")


def _merge_pieces(first, p_ref, pprev_ref, pmeta_ref, sa_ref, sb_ref, o_gated, x_ref,
                  poolw_ref, pscale_ref, wpa_ref, wpb_ref, wout_ref, nw_ref, out_ref, hn_ref):
    d = x_ref.shape[1]
    unit = MERGE_UNIT_COLS
    cols = range(0, d, unit)
    val = {}

    def pool():
        prev = jnp.where(first, pmeta_ref[...], pprev_ref[...])
        p = p_ref[...]
        ext = jnp.concatenate([prev, p], axis=0)
        gd = POOL_GROUP_DIM
        zb = []
        for gi, win in enumerate(POOL_WINDOWS):
            acc = ext[:, gi * gd:(gi + 1) * gd]
            s = 1
            while s < win:
                acc = acc + pltpu.roll(acc, s, axis=0)
                s *= 2
            pooled = acc[N_META:] * (1.0 / win) - p[:, gi * gd:(gi + 1) * gd]
            zb.append(_dot(pooled.astype(_bf16), poolw_ref[gi]))
        val["zb"] = (jnp.concatenate(zb, axis=1) * pscale_ref[...]).astype(_bf16)

    def branch(name, lhs, w_ref, gate_ref, c0):
        def run():
            y = _dot(lhs(), w_ref[:, c0:c0 + unit])
            val[name, c0] = gate_ref[:, c0:c0 + unit].astype(_f32) * y
        return run

    def out_cols(c0):
        def run():
            if "merged" not in val:
                merged = [val["a", m0] + val["b", m0] for m0 in cols]
                val["merged"] = jnp.concatenate(merged, axis=1).astype(_bf16)
            val["h", c0] = x_ref[:, c0:c0 + unit] + _dot(val["merged"], wout_ref[:, c0:c0 + unit])
        return run

    def finish():
        h1 = jnp.concatenate([val["h", c0] for c0 in cols], axis=1)
        out_ref[...] = h1
        hn_ref[...] = _rms_norm(h1, nw_ref[...]).astype(hn_ref.dtype)

    branches = ([branch("b", lambda: val["zb"], wpb_ref, sb_ref, c0) for c0 in cols]
                + [branch("a", o_gated, wpa_ref, sa_ref, c0) for c0 in cols])
    return [pool] + branches, [out_cols(c0) for c0 in cols] + [finish]


def _mixer_kernel(nblk, q_ref, k_ref, v_ref, zs_ref, ba_ref, mq_ref, mk_ref, mv_ref, mba_ref,
                  cwq_ref, cwk_ref, cwv_ref, alog_ref, dtb_ref, onw_ref,
                  p_ref, pprev_ref, pmeta_ref, sa_ref, sb_ref, x_ref,
                  poolw_ref, pscale_ref, wpa_ref, wpb_ref, wout_ref, nw_ref,
                  out_ref, hn_ref, state_ref, hq_ref, hk_ref, hv_ref, obuf_ref):
    c = DELTA_CHUNK
    hd = HEAD_DIM
    heads = range(N_HEADS)
    s = pl.program_id(1)

    def delta_step(q_raw, k_raw, v_raw, ba, hq, hk, hv, valid, states):
        qc = _conv_silu(q_raw, hq, cwq_ref[...])
        yield
        kc = _conv_silu(k_raw, hk, cwk_ref[...])
        yield
        vc = _conv_silu(v_raw, hv, cwv_ref[...])
        yield
        beta_all = _sigmoid(ba)
        g_all = -jnp.exp(alog_ref[...]) * _softplus(ba + dtb_ref[...])
        if valid is not None:
            beta_all = jnp.where(valid, beta_all, 0.0)
            g_all = jnp.where(valid, g_all, 0.0)
        gc_all = _chunk_cumsum(g_all)
        probs = []
        for h in heads:
            sl = slice(h * hd, (h + 1) * hd)
            probs.append((_l2_norm(qc[:, sl]), _l2_norm(kc[:, sl]), vc[:, sl],
                          _lane_column(beta_all, h), _lane_column(gc_all, N_HEADS + h)))
        yield
        return (yield from _delta_chunks(probs, states))

    @pl.when(s == 0)
    def _():
        mq, mk, mv = mq_ref[...], mk_ref[...], mv_ref[...]
        zeros = jnp.zeros((HIST_ROWS, N_HEADS * hd), _f32)
        is_meta = lax.broadcasted_iota(jnp.int32, (c, 128), 0) >= c - N_META
        _, states = _run_interleaved(delta_step(
            mq, mk, mv, mba_ref[...], zeros, zeros, zeros, is_meta,
            [jnp.zeros((hd, hd), _f32) for _ in heads]))
        for h in heads:
            state_ref[h] = states[h]
        hq_ref[...] = mq[c - HIST_ROWS:]
        hk_ref[...] = mk[c - HIST_ROWS:]
        hv_ref[...] = mv[c - HIST_ROWS:]
        obuf_ref[...] = jnp.zeros_like(obuf_ref)

    q_raw, k_raw, v_raw = q_ref[...], k_ref[...], v_ref[...]
    slot = s % 2
    early, late = _merge_pieces(s <= 1, p_ref, pprev_ref, pmeta_ref, sa_ref, sb_ref,
                                lambda: obuf_ref[1 - slot],
                                x_ref, poolw_ref, pscale_ref, wpa_ref, wpb_ref, wout_ref, nw_ref,
                                out_ref, hn_ref)
    base, extra = divmod(len(early), 4)
    outs, states = _run_interleaved(
        delta_step(q_raw, k_raw, v_raw, ba_ref[...], hq_ref[...], hk_ref[...], hv_ref[...], None,
                   [state_ref[h] for h in heads]),
        early + late, plan=[base + 1] * extra + [base] * (4 - extra) + [0, 0, 1] * len(late))
    hq_ref[...] = q_raw[c - HIST_ROWS:]
    hk_ref[...] = k_raw[c - HIST_ROWS:]
    hv_ref[...] = v_raw[c - HIST_ROWS:]
    gated = []
    for h in heads:
        state_ref[h] = states[h]
        o = outs[h]
        o = o * lax.rsqrt(jnp.mean(o * o, axis=-1, keepdims=True) + RMS_EPS)
        sl = slice(h * hd, (h + 1) * hd)
        gated.append((o * onw_ref[...] * zs_ref[:, sl].astype(_f32)).astype(obuf_ref.dtype))
    obuf_ref[slot] = jnp.concatenate(gated, axis=1)


def _mixer(lin, act, ba, meta_lin, meta_lin_pad, meta_ba_pad, conv_w, alog_row, dtb_row, onw_row, x,
           pool_w, pool_scale, w_proj_a, w_proj_b, w_out, ffn_norm_w, batch, p_part, zs_part):
    m, d = x.shape
    c = DELTA_CHUNK
    nblk = m // batch // c
    hd = HEAD_DIM
    w = N_HEADS * hd
    pd = pool_w.shape[0] * POOL_GROUP_DIM
    assert pd == w
    sub = c // N_META

    def delta_row(b, s):
        return b * nblk + jnp.minimum(s, nblk - 1)

    def merge_row(b, s):
        return b * nblk + jnp.maximum(s - 1, 0)

    def tok(part):
        return pl.BlockSpec((c, w), lambda b, s: (delta_row(b, s), part))

    def meta(part):
        return pl.BlockSpec((c, w), lambda b, s: (0, part))

    def cw(part):
        return pl.BlockSpec((CONV_K, w), lambda b, s: (0, part))

    def const(shape):
        return pl.BlockSpec(shape, lambda b, s: (0,) * len(shape), pipeline_mode=pl.Buffered(1))

    row128 = pl.BlockSpec((1, 128), lambda b, s: (0, 0))
    return pl.pallas_call(
        functools.partial(_mixer_kernel, nblk),
        grid=(batch, nblk + 1),
        in_specs=[
            tok(0), tok(1), tok(2), tok(zs_part),
            pl.BlockSpec((c, 128), lambda b, s: (delta_row(b, s), 0)),
            meta(0), meta(1), meta(2),
            pl.BlockSpec((c, 128), lambda b, s: (0, 0)),
            cw(0), cw(1), cw(2),
            row128, row128, row128,
            pl.BlockSpec((c, pd), lambda b, s: (merge_row(b, s), p_part)),
            pl.BlockSpec((N_META, pd), lambda b, s: (jnp.maximum(merge_row(b, s) * sub - 1, 0), p_part)),
            pl.BlockSpec((N_META, pd), lambda b, s: (0, p_part)),
            pl.BlockSpec((c, d), lambda b, s: (merge_row(b, s), 0)),
            pl.BlockSpec((c, d), lambda b, s: (merge_row(b, s), 1)),
            pl.BlockSpec((c, d), lambda b, s: (merge_row(b, s), 0)),
            const(pool_w.shape), const((1, pd)), const(w_proj_a.shape), const(w_proj_b.shape),
            const(w_out.shape), const((1, d)),
        ],
        out_specs=[pl.BlockSpec((c, d), lambda b, s: (merge_row(b, s), 0)),
                   pl.BlockSpec((c, d), lambda b, s: (merge_row(b, s), 0))],
        out_shape=[jax.ShapeDtypeStruct((m, d), _f32), jax.ShapeDtypeStruct((m, d), _bf16)],
        scratch_shapes=[
            pltpu.VMEM((N_HEADS, hd, hd), _f32),
            pltpu.VMEM((HIST_ROWS, w), _f32),
            pltpu.VMEM((HIST_ROWS, w), _f32),
            pltpu.VMEM((HIST_ROWS, w), _f32),
            pltpu.VMEM((2, c, w), _bf16),
        ],
        compiler_params=pltpu.CompilerParams(
            dimension_semantics=("arbitrary", "arbitrary"),
            vmem_limit_bytes=_V7X_VMEM_LIMIT),
        name="mixer",
    )(lin, lin, lin, act, ba, meta_lin_pad, meta_lin_pad, meta_lin_pad, meta_ba_pad,
      conv_w, conv_w, conv_w, alog_row, dtb_row, onw_row,
      lin, lin, meta_lin, act, act, x,
      pool_w, pool_scale, w_proj_a, w_proj_b, w_out, ffn_norm_w)


def _ffn_kernel(h_hbm, hn_ref, wg_ref, wu_ref, wd_ref, fw_ref, out_ref, h_buf, h_sem):
    i = pl.program_id(0)
    f = pl.program_id(1)
    tm = out_ref.shape[0]

    def residual_copy():
        rows = pl.ds(pl.multiple_of(i * tm, tm), tm)
        return pltpu.make_async_copy(h_hbm.at[rows], h_buf, h_sem)

    @pl.when(f == 0)
    def _():
        residual_copy().start()
        out_ref[...] = jnp.zeros_like(out_ref)

    d = out_ref.shape[1]
    rc, dc = min(tm, 512), min(d, 512)
    for r0 in range(0, tm, rc):
        hn = hn_ref[r0:r0 + rc, :]
        act = (_silu(_dot(hn, wg_ref[...])) * _dot(hn, wu_ref[...])).astype(_bf16)
        for c0 in range(0, d, dc):
            out_ref[r0:r0 + rc, c0:c0 + dc] += _dot(act, wd_ref[:, c0:c0 + dc])

    @pl.when(f == pl.num_programs(1) - 1)
    def _():
        residual_copy().wait()
        out_ref[...] = _rms_norm(h_buf[...] + out_ref[...], fw_ref[...])


def _ffn(h, hn, w_gate, w_up, w_down, final_w, tm, tf):
    m, d = h.shape
    f = w_gate.shape[1]
    return pl.pallas_call(
        _ffn_kernel,
        grid=(m // tm, f // tf),
        in_specs=[
            pl.BlockSpec(memory_space=pl.ANY),
            pl.BlockSpec((tm, d), lambda i, j: (i, 0)),
            pl.BlockSpec((d, tf), lambda i, j: (0, j)),
            pl.BlockSpec((d, tf), lambda i, j: (0, j)),
            pl.BlockSpec((tf, d), lambda i, j: (j, 0)),
            pl.BlockSpec((1, d), lambda i, j: (0, 0)),
        ],
        out_specs=pl.BlockSpec((tm, d), lambda i, j: (i, 0)),
        out_shape=jax.ShapeDtypeStruct((m, d), _f32),
        scratch_shapes=[pltpu.VMEM((tm, d), _f32), pltpu.SemaphoreType.DMA(())],
        compiler_params=pltpu.CompilerParams(
            dimension_semantics=("arbitrary", "arbitrary"),
            vmem_limit_bytes=_V7X_VMEM_LIMIT),
        name="ffn",
    )(h, hn, w_gate, w_up, w_down, final_w)


def _largest_tile(n, cap):
    t = cap
    while n % t:
        t //= 2
    return t


def kernel(x, meta_tokens, norm_mix_w, w_in, conv_w, a_log, dt_bias, o_norm_w, w_proj_a, pool_w,
           pool_scale, w_proj_b, w_out, norm_ffn_w, w_ffn_gate, w_ffn_up, w_ffn_down, norm_final_w):
    assert w_in.shape[0] == 1, "single layer block"
    batch, seq, d = x.shape
    h, hd = N_HEADS, HEAD_DIM
    qk = h * hd
    pd = pool_w.shape[1] * POOL_GROUP_DIM
    assert seq % DELTA_CHUNK == 0 and N_META <= DELTA_CHUNK
    m = batch * seq

    w_t = w_in[0].T
    off_z = 3 * qk
    off_ba = 4 * qk
    off_p = off_ba + 2 * h
    off_g = off_p + pd
    tn = qk
    lin_cols, sig_cols = off_z + pd, 2 * d
    assert pd == qk and sig_cols % qk == 0
    row_offsets = ([0, qk, 2 * qk, off_p] + [off_g + c0 for c0 in range(0, sig_cols, tn)] + [off_z])
    w_main_t = _w_prep(w_t, row_offsets, tn, _largest_tile(d, 1024))
    w_ba_t = jnp.pad(w_t[off_ba:off_p], ((0, 128 - 2 * h), (0, 0))).astype(_bf16)
    p_part, zs_part = off_z // pd, sig_cols // qk

    x2 = x.reshape(m, d)
    nw = norm_mix_w[0].reshape(1, d)
    lin, act, ba, wg_bf, wu_bf, wd_bf = _in_proj(
        x2, nw, w_main_t, w_ba_t, _largest_tile(m, 1024), tn, lin_cols, sig_cols,
        casts=(w_ffn_gate[0], w_ffn_up[0], w_ffn_down[0]))
    meta_lin, _, meta_ba = _in_proj(meta_tokens, nw, w_main_t, w_ba_t, N_META, tn, lin_cols, sig_cols)
    pad = ((DELTA_CHUNK - N_META, 0), (0, 0))
    meta_lin_pad = jnp.pad(meta_lin, pad)
    meta_ba_pad = jnp.pad(meta_ba, pad)

    lane_pad = lambda a: jnp.pad(a.reshape(1, h), ((0, 0), (h, 128 - 2 * h)))
    h1, hn2 = _mixer(lin, act, ba, meta_lin, meta_lin_pad, meta_ba_pad, conv_w[0], lane_pad(a_log[0]),
                     lane_pad(dt_bias[0]), o_norm_w[0].reshape(1, hd), x2, pool_w[0].astype(_bf16),
                     pool_scale[0].reshape(1, pd), w_proj_a[0].astype(_bf16),
                     w_proj_b[0].astype(_bf16), w_out[0].astype(_bf16), norm_ffn_w[0].reshape(1, d),
                     batch, p_part, zs_part)

    out = _ffn(h1, hn2, wg_bf, wu_bf, wd_bf, norm_final_w.reshape(1, d), _largest_tile(m, 1024),
               _largest_tile(w_ffn_gate.shape[2], 512))
    return out.reshape(batch, seq, d)
```

```python
import functools

import jax
import jax.numpy as jnp
from jax import lax
from jax.experimental import pallas as pl
from jax.experimental.pallas import tpu as pltpu

N_META = 16
N_HEADS = 8
HEAD_DIM = 128
CONV_K = 4
POOL_WINDOWS = (2, 4, 8, 16)
POOL_GROUP_DIM = 256
RMS_EPS = 1e-6
L2_EPS = 1e-6

DELTA_CHUNK = 128
INV_BASE = 16
HIST_ROWS = 8

_V7X_VMEM_LIMIT = 56 * 1024 * 1024

_f32 = jnp.float32
_bf16 = jnp.bfloat16


def _dot(a, b):
    return jnp.dot(a, b, preferred_element_type=_f32)


def _dot_nt(a, b):
    return lax.dot_general(a, b, (((1,), (1,)), ((), ())), preferred_element_type=_f32)


def _sigmoid(x):
    return 1.0 / (1.0 + jnp.exp(-x))


def _silu(x):
    return x * _sigmoid(x)


def _softplus(x):
    return jnp.maximum(x, 0.0) + jnp.log1p(jnp.exp(-jnp.abs(x)))


def _rms_norm(x, w):
    return x * lax.rsqrt(jnp.mean(x * x, axis=-1, keepdims=True) + RMS_EPS) * w


def _w_prep_kernel(w_ref, o_ref):
    o_ref[...] = w_ref[...].astype(o_ref.dtype)


def _w_prep(w_t, row_offsets, tn, tk):
    n_src, k = w_t.shape
    assert all(off % 8 == 0 and off + tn <= n_src for off in row_offsets)

    def row_off(j):
        tile = 0
        for jj, o in enumerate(row_offsets):
            tile = jnp.where(j == jj, o // 8, tile)
        return tile * 8

    return pl.pallas_call(
        _w_prep_kernel,
        grid=(len(row_offsets), k // tk),
        in_specs=[pl.BlockSpec((pl.Element(tn), pl.Element(tk)),
                               lambda j, c: (row_off(j), c * tk))],
        out_specs=pl.BlockSpec((tn, tk), lambda j, c: (j, c)),
        out_shape=jax.ShapeDtypeStruct((len(row_offsets) * tn, k), _bf16),
        compiler_params=pltpu.CompilerParams(
            dimension_semantics=("arbitrary", "arbitrary"),
            vmem_limit_bytes=_V7X_VMEM_LIMIT),
        name="w_prep",
    )(w_t)


CAST_COL_SPLIT = 4


def _sigmoid_tanh(x):
    return 0.5 * jnp.tanh(0.5 * x) + 0.5


def _in_proj_kernel(n_lin, n_sig, lin_only, cast_starts, x_ref, nw_ref, wba_ref, w_ref, *refs):
    n_cast = len(cast_starts)
    cast_src, refs = refs[:n_cast], list(refs[n_cast:])
    lin_ref = refs.pop(0)
    act_ref = None if lin_only else refs.pop(0)
    ba_ref = refs.pop(0)
    cast_dst, hn_ref = refs[:n_cast], refs[n_cast]
    j = pl.program_id(1)

    @pl.when(j == 0)
    def _():
        hn = _rms_norm(x_ref[...], nw_ref[...]).astype(_bf16)
        hn_ref[...] = hn
        ba_ref[...] = _dot_nt(hn, wba_ref[...])

    @pl.when(j < n_lin)
    def _():
        lin_ref[...] = _dot_nt(hn_ref[...], w_ref[...])

    if not lin_only:
        @pl.when(jnp.logical_and(j >= n_lin, j < n_lin + n_sig))
        def _():
            act_ref[...] = _sigmoid_tanh(_dot_nt(hn_ref[...], w_ref[...])).astype(act_ref.dtype)

        @pl.when(j >= n_lin + n_sig)
        def _():
            y = _dot_nt(hn_ref[...], w_ref[...])
            act_ref[...] = (y * _sigmoid_tanh(y)).astype(act_ref.dtype)

    for start, src, dst in zip(cast_starts, cast_src, cast_dst):
        @pl.when(jnp.logical_and(j >= start, j < start + CAST_COL_SPLIT))
        def _(src=src, dst=dst):
            dst[...] = src[...].astype(dst.dtype)


def _in_proj(x, norm_w, w_main_t, w_ba_t, tm, tn, lin_cols, sig_cols, casts=(), lin_only=False):
    m, d = x.shape
    n = lin_cols if lin_only else w_main_t.shape[0]
    assert lin_cols % tn == 0 and sig_cols % tn == 0 and n % tn == 0
    n_lin, n_sig = lin_cols // tn, sig_cols // tn
    gi, gj = m // tm, n // tn
    assert gj >= CAST_COL_SPLIT or not casts
    cast_starts = tuple(min(CAST_COL_SPLIT * k, gj - CAST_COL_SPLIT) for k in range(len(casts)))
    act_spec, act_shape = [], []
    if not lin_only:
        act_spec = [pl.BlockSpec((tm, tn), lambda i, j: (i, jnp.maximum(j - n_lin, 0)))]
        act_shape = [jax.ShapeDtypeStruct((m, n - lin_cols), _bf16)]
    cast_specs = []
    for start, a in zip(cast_starts, casts):
        rows, cols = a.shape
        assert rows % (16 * gi) == 0 and cols % (128 * CAST_COL_SPLIT) == 0
        cast_specs.append(pl.BlockSpec(
            (rows // gi, cols // CAST_COL_SPLIT),
            lambda i, j, start=start: (i, jnp.clip(j - start, 0, CAST_COL_SPLIT - 1))))
    return pl.pallas_call(
        functools.partial(_in_proj_kernel, n_lin, n_sig, lin_only, cast_starts),
        grid=(gi, gj),
        in_specs=[
            pl.BlockSpec((tm, d), lambda i, j: (i, 0)),
            pl.BlockSpec((1, d), lambda i, j: (0, 0)),
            pl.BlockSpec((128, d), lambda i, j: (0, 0)),
            pl.BlockSpec((tn, d), lambda i, j: (j, 0)),
        ] + cast_specs,
        out_specs=[
            pl.BlockSpec((tm, tn), lambda i, j: (i, jnp.minimum(j, n_lin - 1))),
        ] + act_spec + [
            pl.BlockSpec((tm, 128), lambda i, j: (i, 0)),
        ] + cast_specs,
        out_shape=[jax.ShapeDtypeStruct((m, lin_cols), _f32)] + act_shape + [
            jax.ShapeDtypeStruct((m, 128), _f32),
        ] + [jax.ShapeDtypeStruct(a.shape, _bf16) for a in casts],
        scratch_shapes=[pltpu.VMEM((tm, d), _bf16)],
        compiler_params=pltpu.CompilerParams(
            dimension_semantics=("arbitrary", "arbitrary"),
            vmem_limit_bytes=_V7X_VMEM_LIMIT),
        name="in_proj",
    )(x, norm_w, w_ba_t, w_main_t, *casts)


def _conv_silu(x, hist, cw):
    assert CONV_K == 4
    ext = jnp.concatenate([hist, x], axis=0)
    ext1 = pltpu.roll(ext, 1, axis=0)
    old = ext * cw[1:2] + ext1 * cw[0:1]
    acc = ext * cw[3:4] + ext1 * cw[2:3] + pltpu.roll(old, 2, axis=0)
    return _silu(acc[HIST_ROWS:])


def _l2_norm(x):
    return x * lax.rsqrt(jnp.sum(x * x, axis=-1, keepdims=True) + L2_EPS)


def _chunk_cumsum(g):
    pos = lax.broadcasted_iota(jnp.int32, g.shape, 0) % DELTA_CHUNK
    s = 1
    while s < DELTA_CHUNK:
        g = g + jnp.where(pos >= s, pltpu.roll(g, s, axis=0), 0.0)
        s *= 2
    return g


def _lane_column(x, lane_idx):
    lane = lax.broadcasted_iota(jnp.int32, x.shape, 1)
    col = jnp.sum(jnp.where(lane == lane_idx, x, 0.0), axis=1, keepdims=True)
    return jnp.broadcast_to(col, x.shape)


LEVEL_GROUP = 4
MERGE_UNIT_COLS = 256


def _level(thunks):
    out = []
    for i, thunk in enumerate(thunks):
        out.append(thunk())
        if i % LEVEL_GROUP == LEVEL_GROUP - 1 or i == len(thunks) - 1:
            yield
    return out


def _unit_lower_inverse(mats):
    c = mats[0].shape[0]
    row = lax.broadcasted_iota(jnp.int32, (c, c), 0)
    col = lax.broadcasted_iota(jnp.int32, (c, c), 1)
    apart = row ^ col
    eye = jnp.where(row == col, 1.0, 0.0)
    diag = [jnp.where(apart < INV_BASE, a, 0.0) for a in mats]
    ts = [eye - a for a in diag]
    apows = yield from _level([lambda a=a: _dot(a.astype(_bf16), a.astype(_bf16)) for a in diag])
    n = 2
    while n < INV_BASE:
        apbs = [ap.astype(_bf16) for ap in apows]
        if 2 * n < INV_BASE:
            both = yield from _level(
                [lambda t=t, apb=apb: _dot(jnp.concatenate([t.astype(_bf16), apb], axis=0), apb)
                 for t, apb in zip(ts, apbs)])
            ts = [t + b[:c] for t, b in zip(ts, both)]
            apows = [b[c:] for b in both]
        else:
            ts = yield from _level([lambda t=t, apb=apb: t + _dot(t.astype(_bf16), apb)
                                    for t, apb in zip(ts, apbs)])
        n *= 2
    size = INV_BASE
    while size < c:
        lower_left = jnp.logical_and(apart >= size, apart < 2 * size)
        tbs = [t.astype(_bf16) for t in ts]
        xs = yield from _level([lambda a=a, tb=tb: _dot(jnp.where(lower_left, a, 0.0).astype(_bf16), tb)
                                for a, tb in zip(mats, tbs)])
        ts = yield from _level([lambda t=t, tb=tb, x=x: t - _dot(tb, x.astype(_bf16))
                                for t, tb, x in zip(ts, tbs, xs)])
        size *= 2
    return ts


def _delta_chunks(probs, states):
    c = DELTA_CHUNK
    row = lax.broadcasted_iota(jnp.int32, (c, c), 0)
    col = lax.broadcasted_iota(jnp.int32, (c, c), 1)
    qn, kn, v, beta, gc = (list(t) for t in zip(*probs))
    n = len(probs)
    rng = range(n)
    decay = [jnp.exp(jnp.where(row >= col, gc[i] - gc[i].T, -jnp.inf)) for i in rng]
    qs = [qn[i] * (HEAD_DIM ** -0.5) for i in rng]
    kb = [kn[i].astype(_bf16) for i in rng]
    qk_kk = yield from _level(
        [lambda i=i: _dot_nt(jnp.concatenate([qs[i].astype(_bf16), kb[i]], axis=0), kb[i]) for i in rng])
    qk = [(qk_kk[i][:c] * decay[i]).astype(_bf16) for i in rng]
    a_mat = [jnp.where(row > col, qk_kk[i][c:] * beta[i] * decay[i], 0.0) for i in rng]
    t_mat = yield from _unit_lower_inverse(a_mat)
    eg = [jnp.exp(gc[i]) for i in rng]
    rhs = [jnp.concatenate([v[i] * beta[i], kn[i] * (beta[i] * eg[i])], axis=1).astype(_bf16)
           for i in rng]
    uw = yield from _level([lambda i=i: _dot(t_mat[i].astype(_bf16), rhs[i]) for i in rng])
    g_last = [gc[i][c - 1:c] for i in rng]
    k_dec_t = [(kn[i] * jnp.exp(g_last[i] - gc[i])).T.astype(_bf16) for i in rng]
    wq = [jnp.concatenate([uw[i][:, HEAD_DIM:], qs[i] * eg[i]], axis=0).astype(_bf16) for i in rng]
    ws_qs = yield from _level([lambda i=i: _dot(wq[i], states[i].astype(_bf16)) for i in rng])
    vb = [(uw[i][:, :HEAD_DIM] - ws_qs[i][:c]).astype(_bf16) for i in rng]
    outs = yield from _level([lambda i=i: ws_qs[i][c:] + _dot(qk[i], vb[i]) for i in rng])
    new_states = yield from _level(
        [lambda i=i: states[i] * jnp.exp(g_last[i]) + _dot(k_dec_t[i], vb[i]) for i in rng])
    return outs, new_states


def _run_interleaved(gen, fillers=(), plan=()):
    fillers = list(fillers)
    for count in list(plan) + [0] * 64:
        try:
            next(gen)
        except StopIteration as stop:
            for f in fillers:
                f()
            return stop.value
        for _ in range(min(count, len(fillers))):
            fillers.pop(0)()
    raise AssertionError("generator yielded more often than planned for")


def _merge_pieces(first, p_ref, pprev_ref, pmeta_ref, sa_ref, sb_ref, o_gated, x_ref,
                  poolw_ref, pscale_ref, wpa_ref, wpb_ref, wout_ref, nw_ref, out_ref, hn_ref):
    d = x_ref.shape[1]
    unit = MERGE_UNIT_COLS
    cols = range(0, d, unit)
    val = {}

    def pool():
        prev = jnp.where(first, pmeta_ref[...], pprev_ref[...])
        p = p_ref[...]
        ext = jnp.concatenate([prev, p], axis=0)
        gd = POOL_GROUP_DIM
        zb = []
        for gi, win in enumerate(POOL_WINDOWS):
            acc = ext[:, gi * gd:(gi + 1) * gd]
            s = 1
            while s < win:
                acc = acc + pltpu.roll(acc, s, axis=0)
                s *= 2
            pooled = acc[N_META:] * (1.0 / win) - p[:, gi * gd:(gi + 1) * gd]
            zb.append(_dot(pooled.astype(_bf16), poolw_ref[gi]))
        val["zb"] = (jnp.concatenate(zb, axis=1) * pscale_ref[...]).astype(_bf16)

    def branch(name, lhs, w_ref, gate_ref, c0):
        def run():
            y = _dot(lhs(), w_ref[:, c0:c0 + unit])
            val[name, c0] = gate_ref[:, c0:c0 + unit].astype(_f32) * y
        return run

    def out_cols(c0):
        def run():
            if "merged" not in val:
                merged = [val["a", m0] + val["b", m0] for m0 in cols]
                val["merged"] = jnp.concatenate(merged, axis=1).astype(_bf16)
            val["h", c0] = x_ref[:, c0:c0 + unit] + _dot(val["merged"], wout_ref[:, c0:c0 + unit])
        return run

    def finish():
        h1 = jnp.concatenate([val["h", c0] for c0 in cols], axis=1)
        out_ref[...] = h1
        hn_ref[...] = _rms_norm(h1, nw_ref[...]).astype(hn_ref.dtype)

    branches = ([branch("b", lambda: val["zb"], wpb_ref, sb_ref, c0) for c0 in cols]
                + [branch("a", o_gated, wpa_ref, sa_ref, c0) for c0 in cols])
    return [pool] + branches, [out_cols(c0) for c0 in cols] + [finish]


def _mixer_kernel(nblk, q_ref, k_ref, v_ref, zs_ref, ba_ref, mq_ref, mk_ref, mv_ref, mba_ref,
                  cwq_ref, cwk_ref, cwv_ref, alog_ref, dtb_ref, onw_ref,
                  p_ref, pprev_ref, pmeta_ref, sa_ref, sb_ref, x_ref,
                  poolw_ref, pscale_ref, wpa_ref, wpb_ref, wout_ref, nw_ref,
                  out_ref, hn_ref, state_ref, hq_ref, hk_ref, hv_ref, obuf_ref):
    c = DELTA_CHUNK
    hd = HEAD_DIM
    heads = range(N_HEADS)
    s = pl.program_id(1)

    def delta_step(q_raw, k_raw, v_raw, ba, hq, hk, hv, valid, states):
        qc = _conv_silu(q_raw, hq, cwq_ref[...])
        yield
        kc = _conv_silu(k_raw, hk, cwk_ref[...])
        yield
        vc = _conv_silu(v_raw, hv, cwv_ref[...])
        yield
        beta_all = _sigmoid(ba)
        g_all = -jnp.exp(alog_ref[...]) * _softplus(ba + dtb_ref[...])
        if valid is not None:
            beta_all = jnp.where(valid, beta_all, 0.0)
            g_all = jnp.where(valid, g_all, 0.0)
        gc_all = _chunk_cumsum(g_all)
        probs = []
        for h in heads:
            sl = slice(h * hd, (h + 1) * hd)
            probs.append((_l2_norm(qc[:, sl]), _l2_norm(kc[:, sl]), vc[:, sl],
                          _lane_column(beta_all, h), _lane_column(gc_all, N_HEADS + h)))
        yield
        return (yield from _delta_chunks(probs, states))

    @pl.when(s == 0)
    def _():
        mq, mk, mv = mq_ref[...], mk_ref[...], mv_ref[...]
        zeros = jnp.zeros((HIST_ROWS, N_HEADS * hd), _f32)
        is_meta = lax.broadcasted_iota(jnp.int32, (c, 128), 0) >= c - N_META
        _, states = _run_interleaved(delta_step(
            mq, mk, mv, mba_ref[...], zeros, zeros, zeros, is_meta,
            [jnp.zeros((hd, hd), _f32) for _ in heads]))
        for h in heads:
            state_ref[h] = states[h]
        hq_ref[...] = mq[c - HIST_ROWS:]
        hk_ref[...] = mk[c - HIST_ROWS:]
        hv_ref[...] = mv[c - HIST_ROWS:]
        obuf_ref[...] = jnp.zeros_like(obuf_ref)

    q_raw, k_raw, v_raw = q_ref[...], k_ref[...], v_ref[...]
    slot = s % 2
    early, late = _merge_pieces(s <= 1, p_ref, pprev_ref, pmeta_ref, sa_ref, sb_ref,
                                lambda: obuf_ref[1 - slot],
                                x_ref, poolw_ref, pscale_ref, wpa_ref, wpb_ref, wout_ref, nw_ref,
                                out_ref, hn_ref)
    base, extra = divmod(len(early), 4)
    outs, states = _run_interleaved(
        delta_step(q_raw, k_raw, v_raw, ba_ref[...], hq_ref[...], hk_ref[...], hv_ref[...], None,
                   [state_ref[h] for h in heads]),
        early + late, plan=[base + 1] * extra + [base] * (4 - extra) + [0, 0, 1] * len(late))
    hq_ref[...] = q_raw[c - HIST_ROWS:]
    hk_ref[...] = k_raw[c - HIST_ROWS:]
    hv_ref[...] = v_raw[c - HIST_ROWS:]
    gated = []
    for h in heads:
        state_ref[h] = states[h]
        o = outs[h]
        o = o * lax.rsqrt(jnp.mean(o * o, axis=-1, keepdims=True) + RMS_EPS)
        sl = slice(h * hd, (h + 1) * hd)
        gated.append((o * onw_ref[...] * zs_ref[:, sl].astype(_f32)).astype(obuf_ref.dtype))
    obuf_ref[slot] = jnp.concatenate(gated, axis=1)


def _mixer(lin, act, ba, meta_lin, meta_lin_pad, meta_ba_pad, conv_w, alog_row, dtb_row, onw_row, x,
           pool_w, pool_scale, w_proj_a, w_proj_b, w_out, ffn_norm_w, batch, p_part, zs_part):
    m, d = x.shape
    c = DELTA_CHUNK
    nblk = m // batch // c
    hd = HEAD_DIM
    w = N_HEADS * hd
    pd = pool_w.shape[0] * POOL_GROUP_DIM
    assert pd == w
    sub = c // N_META

    def delta_row(b, s):
        return b * nblk + jnp.minimum(s, nblk - 1)

    def merge_row(b, s):
        return b * nblk + jnp.maximum(s - 1, 0)

    def tok(part):
        return pl.BlockSpec((c, w), lambda b, s: (delta_row(b, s), part))

    def meta(part):
        return pl.BlockSpec((c, w), lambda b, s: (0, part))

    def cw(part):
        return pl.BlockSpec((CONV_K, w), lambda b, s: (0, part))

    def const(shape):
        return pl.BlockSpec(shape, lambda b, s: (0,) * len(shape), pipeline_mode=pl.Buffered(1))

    row128 = pl.BlockSpec((1, 128), lambda b, s: (0, 0))
    return pl.pallas_call(
        functools.partial(_mixer_kernel, nblk),
        grid=(batch, nblk + 1),
        in_specs=[
            tok(0), tok(1), tok(2), tok(zs_part),
            pl.BlockSpec((c, 128), lambda b, s: (delta_row(b, s), 0)),
            meta(0), meta(1), meta(2),
            pl.BlockSpec((c, 128), lambda b, s: (0, 0)),
            cw(0), cw(1), cw(2),
            row128, row128, row128,
            pl.BlockSpec((c, pd), lambda b, s: (merge_row(b, s), p_part)),
            pl.BlockSpec((N_META, pd), lambda b, s: (jnp.maximum(merge_row(b, s) * sub - 1, 0), p_part)),
            pl.BlockSpec((N_META, pd), lambda b, s: (0, p_part)),
            pl.BlockSpec((c, d), lambda b, s: (merge_row(b, s), 0)),
            pl.BlockSpec((c, d), lambda b, s: (merge_row(b, s), 1)),
            pl.BlockSpec((c, d), lambda b, s: (merge_row(b, s), 0)),
            const(pool_w.shape), const((1, pd)), const(w_proj_a.shape), const(w_proj_b.shape),
            const(w_out.shape), const((1, d)),
        ],
        out_specs=[pl.BlockSpec((c, d), lambda b, s: (merge_row(b, s), 0)),
                   pl.BlockSpec((c, d), lambda b, s: (merge_row(b, s), 0))],
        out_shape=[jax.ShapeDtypeStruct((m, d), _f32), jax.ShapeDtypeStruct((m, d), _bf16)],
        scratch_shapes=[
            pltpu.VMEM((N_HEADS, hd, hd), _f32),
            pltpu.VMEM((HIST_ROWS, w), _f32),
            pltpu.VMEM((HIST_ROWS, w), _f32),
            pltpu.VMEM((HIST_ROWS, w), _f32),
            pltpu.VMEM((2, c, w), _bf16),
        ],
        compiler_params=pltpu.CompilerParams(
            dimension_semantics=("arbitrary", "arbitrary"),
            vmem_limit_bytes=_V7X_VMEM_LIMIT),
        name="mixer",
    )(lin, lin, lin, act, ba, meta_lin_pad, meta_lin_pad, meta_lin_pad, meta_ba_pad,
      conv_w, conv_w, conv_w, alog_row, dtb_row, onw_row,
      lin, lin, meta_lin, act, act, x,
      pool_w, pool_scale, w_proj_a, w_proj_b, w_out, ffn_norm_w)


def _ffn_kernel(h_hbm, hn_ref, wg_ref, wu_ref, wd_ref, fw_ref, out_ref, h_buf, h_sem):
    i = pl.program_id(0)
    f = pl.program_id(1)
    tm = out_ref.shape[0]

    def residual_copy():
        rows = pl.ds(pl.multiple_of(i * tm, tm), tm)
        return pltpu.make_async_copy(h_hbm.at[rows], h_buf, h_sem)

    @pl.when(f == 0)
    def _():
        residual_copy().start()
        out_ref[...] = jnp.zeros_like(out_ref)

    d = out_ref.shape[1]
    rc, dc = min(tm, 512), min(d, 512)
    for r0 in range(0, tm, rc):
        hn = hn_ref[r0:r0 + rc, :]
        act = (_silu(_dot(hn, wg_ref[...])) * _dot(hn, wu_ref[...])).astype(_bf16)
        for c0 in range(0, d, dc):
            out_ref[r0:r0 + rc, c0:c0 + dc] += _dot(act, wd_ref[:, c0:c0 + dc])

    @pl.when(f == pl.num_programs(1) - 1)
    def _():
        residual_copy().wait()
        out_ref[...] = _rms_norm(h_buf[...] + out_ref[...], fw_ref[...])


def _ffn(h, hn, w_gate, w_up, w_down, final_w, tm, tf):
    m, d = h.shape
    f = w_gate.shape[1]
    return pl.pallas_call(
        _ffn_kernel,
        grid=(m // tm, f // tf),
        in_specs=[
            pl.BlockSpec(memory_space=pl.ANY),
            pl.BlockSpec((tm, d), lambda i, j: (i, 0)),
            pl.BlockSpec((d, tf), lambda i, j: (0, j)),
            pl.BlockSpec((d, tf), lambda i, j: (0, j)),
            pl.BlockSpec((tf, d), lambda i, j: (j, 0)),
            pl.BlockSpec((1, d), lambda i, j: (0, 0)),
        ],
        out_specs=pl.BlockSpec((tm, d), lambda i, j: (i, 0)),
        out_shape=jax.ShapeDtypeStruct((m, d), _f32),
        scratch_shapes=[pltpu.VMEM((tm, d), _f32), pltpu.SemaphoreType.DMA(())],
        compiler_params=pltpu.CompilerParams(
            dimension_semantics=("arbitrary", "arbitrary"),
            vmem_limit_bytes=_V7X_VMEM_LIMIT),
        name="ffn",
    )(h, hn, w_gate, w_up, w_down, final_w)


def _largest_tile(n, cap):
    t = cap
    while n % t:
        t //= 2
    return t


def kernel(x, meta_tokens, norm_mix_w, w_in, conv_w, a_log, dt_bias, o_norm_w, w_proj_a, pool_w,
           pool_scale, w_proj_b, w_out, norm_ffn_w, w_ffn_gate, w_ffn_up, w_ffn_down, norm_final_w):
    assert w_in.shape[0] == 1, "single layer block"
    batch, seq, d = x.shape
    h, hd = N_HEADS, HEAD_DIM
    qk = h * hd
    pd = pool_w.shape[1] * POOL_GROUP_DIM
    assert seq % DELTA_CHUNK == 0 and N_META <= DELTA_CHUNK
    m = batch * seq

    w_t = w_in[0].T
    off_z = 3 * qk
    off_ba = 4 * qk
    off_p = off_ba + 2 * h
    off_g = off_p + pd
    tn = qk
    lin_cols, sig_cols = off_z + pd, 2 * d
    assert pd == qk and sig_cols % qk == 0
    row_offsets = ([0, qk, 2 * qk, off_p] + [off_g + c0 for c0 in range(0, sig_cols, tn)] + [off_z])
    w_main_t = _w_prep(w_t, row_offsets, tn, _largest_tile(d, 1024))
    w_ba_t = jnp.pad(w_t[off_ba:off_p], ((0, 128 - 2 * h), (0, 0))).astype(_bf16)
    p_part, zs_part = off_z // pd, sig_cols // qk

    x2 = x.reshape(m, d)
    nw = norm_mix_w[0].reshape(1, d)
    meta_lin, meta_ba = _in_proj(meta_tokens, nw, w_main_t, w_ba_t, N_META, tn, lin_cols, sig_cols,
                                 lin_only=True)
    lin, act, ba, wg_bf, wu_bf, wd_bf, wpa_bf, wpb_bf, wout_bf = _in_proj(
        x2, nw, w_main_t, w_ba_t, _largest_tile(m, 1024), tn, lin_cols, sig_cols,
        casts=(w_ffn_gate[0], w_ffn_up[0], w_ffn_down[0], w_proj_a[0], w_proj_b[0], w_out[0]))
    pad = ((DELTA_CHUNK - N_META, 0), (0, 0))
    meta_lin_pad = jnp.pad(meta_lin, pad)
    meta_ba_pad = jnp.pad(meta_ba, pad)

    lane_pad = lambda a: jnp.pad(a.reshape(1, h), ((0, 0), (h, 128 - 2 * h)))
    h1, hn2 = _mixer(lin, act, ba, meta_lin, meta_lin_pad, meta_ba_pad, conv_w[0], lane_pad(a_log[0]),
                     lane_pad(dt_bias[0]), o_norm_w[0].reshape(1, hd), x2, pool_w[0].astype(_bf16),
                     pool_scale[0].reshape(1, pd), wpa_bf, wpb_bf, wout_bf,
                     norm_ffn_w[0].reshape(1, d), batch, p_part, zs_part)

    out = _ffn(h1, hn2, wg_bf, wu_bf, wd_bf, norm_final_w.reshape(1, d), _largest_tile(m, 1024),
               _largest_tile(w_ffn_gate.shape[2], 512))
    return out.reshape(batch, seq, d)
```

```python
import functools

import jax
import jax.numpy as jnp
from jax import lax
from jax.experimental import pallas as pl
from jax.experimental.pallas import tpu as pltpu

N_META = 16
N_HEADS = 8
HEAD_DIM = 128
CONV_K = 4
POOL_WINDOWS = (2, 4, 8, 16)
POOL_GROUP_DIM = 256
RMS_EPS = 1e-6
L2_EPS = 1e-6

DELTA_CHUNK = 128
INV_BASE = 16
HIST_ROWS = 8
MIXER_ROWS = 256

_V7X_VMEM_LIMIT = 56 * 1024 * 1024

_f32 = jnp.float32
_bf16 = jnp.bfloat16


def _dot(a, b):
    return jnp.dot(a, b, preferred_element_type=_f32)


def _dot_nt(a, b):
    return lax.dot_general(a, b, (((1,), (1,)), ((), ())), preferred_element_type=_f32)


def _sigmoid(x):
    return 1.0 / (1.0 + jnp.exp(-x))


def _silu(x):
    return x * _sigmoid(x)


def _softplus(x):
    return jnp.maximum(x, 0.0) + jnp.log1p(jnp.exp(-jnp.abs(x)))


def _rms_norm(x, w):
    return x * lax.rsqrt(jnp.mean(x * x, axis=-1, keepdims=True) + RMS_EPS) * w


def _w_prep_kernel(w_ref, o_ref):
    o_ref[...] = w_ref[...].astype(o_ref.dtype)


def _w_prep(w_t, row_offsets, tn, tk):
    n_src, k = w_t.shape
    assert all(off % 8 == 0 and off + tn <= n_src for off in row_offsets)

    def row_off(j):
        tile = 0
        for jj, o in enumerate(row_offsets):
            tile = jnp.where(j == jj, o // 8, tile)
        return tile * 8

    return pl.pallas_call(
        _w_prep_kernel,
        grid=(len(row_offsets), k // tk),
        in_specs=[pl.BlockSpec((pl.Element(tn), pl.Element(tk)),
                               lambda j, c: (row_off(j), c * tk))],
        out_specs=pl.BlockSpec((tn, tk), lambda j, c: (j, c)),
        out_shape=jax.ShapeDtypeStruct((len(row_offsets) * tn, k), _bf16),
        compiler_params=pltpu.CompilerParams(
            dimension_semantics=("arbitrary", "arbitrary"),
            vmem_limit_bytes=_V7X_VMEM_LIMIT),
        name="w_prep",
    )(w_t)


CAST_COL_SPLIT = 4


def _sigmoid_tanh(x):
    return 0.5 * jnp.tanh(0.5 * x) + 0.5


def _in_proj_kernel(n_lin, n_sig, lin_only, cast_starts, x_ref, nw_ref, wba_ref, w_ref, *refs):
    n_cast = len(cast_starts)
    cast_src, refs = refs[:n_cast], list(refs[n_cast:])
    lin_ref = refs.pop(0)
    act_ref = None if lin_only else refs.pop(0)
    ba_ref = refs.pop(0)
    cast_dst, hn_ref = refs[:n_cast], refs[n_cast]
    j = pl.program_id(1)

    @pl.when(j == 0)
    def _():
        hn = _rms_norm(x_ref[...], nw_ref[...]).astype(_bf16)
        hn_ref[...] = hn
        ba_ref[...] = _dot_nt(hn, wba_ref[...])

    @pl.when(j < n_lin)
    def _():
        lin_ref[...] = _dot_nt(hn_ref[...], w_ref[...])

    if not lin_only:
        @pl.when(jnp.logical_and(j >= n_lin, j < n_lin + n_sig))
        def _():
            act_ref[...] = _sigmoid_tanh(_dot_nt(hn_ref[...], w_ref[...])).astype(act_ref.dtype)

        @pl.when(j >= n_lin + n_sig)
        def _():
            y = _dot_nt(hn_ref[...], w_ref[...])
            act_ref[...] = (y * _sigmoid_tanh(y)).astype(act_ref.dtype)

    for start, src, dst in zip(cast_starts, cast_src, cast_dst):
        @pl.when(jnp.logical_and(j >= start, j < start + CAST_COL_SPLIT))
        def _(src=src, dst=dst):
            dst[...] = src[...].astype(dst.dtype)


def _in_proj(x, norm_w, w_main_t, w_ba_t, tm, tn, lin_cols, sig_cols, casts=(), lin_only=False):
    m, d = x.shape
    n = lin_cols if lin_only else w_main_t.shape[0]
    assert lin_cols % tn == 0 and sig_cols % tn == 0 and n % tn == 0
    n_lin, n_sig = lin_cols // tn, sig_cols // tn
    gi, gj = m // tm, n // tn
    assert gj >= CAST_COL_SPLIT or not casts
    cast_starts = tuple(min(CAST_COL_SPLIT * k, gj - CAST_COL_SPLIT) for k in range(len(casts)))
    act_spec, act_shape = [], []
    if not lin_only:
        act_spec = [pl.BlockSpec((tm, tn), lambda i, j: (i, jnp.maximum(j - n_lin, 0)))]
        act_shape = [jax.ShapeDtypeStruct((m, n - lin_cols), _bf16)]
    cast_specs = []
    for start, a in zip(cast_starts, casts):
        rows, cols = a.shape
        assert rows % (16 * gi) == 0 and cols % (128 * CAST_COL_SPLIT) == 0
        cast_specs.append(pl.BlockSpec(
            (rows // gi, cols // CAST_COL_SPLIT),
            lambda i, j, start=start: (i, jnp.clip(j - start, 0, CAST_COL_SPLIT - 1))))
    return pl.pallas_call(
        functools.partial(_in_proj_kernel, n_lin, n_sig, lin_only, cast_starts),
        grid=(gi, gj),
        in_specs=[
            pl.BlockSpec((tm, d), lambda i, j: (i, 0)),
            pl.BlockSpec((1, d), lambda i, j: (0, 0)),
            pl.BlockSpec((128, d), lambda i, j: (0, 0)),
            pl.BlockSpec((tn, d), lambda i, j: (j, 0)),
        ] + cast_specs,
        out_specs=[
            pl.BlockSpec((tm, tn), lambda i, j: (i, jnp.minimum(j, n_lin - 1))),
        ] + act_spec + [
            pl.BlockSpec((tm, 128), lambda i, j: (i, 0)),
        ] + cast_specs,
        out_shape=[jax.ShapeDtypeStruct((m, lin_cols), _f32)] + act_shape + [
            jax.ShapeDtypeStruct((m, 128), _f32),
        ] + [jax.ShapeDtypeStruct(a.shape, _bf16) for a in casts],
        scratch_shapes=[pltpu.VMEM((tm, d), _bf16)],
        compiler_params=pltpu.CompilerParams(
            dimension_semantics=("arbitrary", "arbitrary"),
            vmem_limit_bytes=_V7X_VMEM_LIMIT),
        name="in_proj",
    )(x, norm_w, w_ba_t, w_main_t, *casts)


def _conv_silu(x, hist, cw):
    assert CONV_K == 4
    ext = jnp.concatenate([hist, x], axis=0)
    ext1 = pltpu.roll(ext, 1, axis=0)
    old = ext * cw[1:2] + ext1 * cw[0:1]
    acc = ext * cw[3:4] + ext1 * cw[2:3] + pltpu.roll(old, 2, axis=0)
    return _silu(acc[HIST_ROWS:])


def _l2_norm(x):
    return x * lax.rsqrt(jnp.sum(x * x, axis=-1, keepdims=True) + L2_EPS)


def _chunk_cumsum(g):
    pos = lax.broadcasted_iota(jnp.int32, g.shape, 0) % DELTA_CHUNK
    s = 1
    while s < DELTA_CHUNK:
        g = g + jnp.where(pos >= s, pltpu.roll(g, s, axis=0), 0.0)
        s *= 2
    return g


def _lane_column(x, lane_idx):
    lane = lax.broadcasted_iota(jnp.int32, x.shape, 1)
    col = jnp.sum(jnp.where(lane == lane_idx, x, 0.0), axis=1, keepdims=True)
    return jnp.broadcast_to(col, x.shape)


LEVEL_GROUP = 4
MERGE_UNIT_COLS = 256


def _level(thunks):
    out = []
    for i, thunk in enumerate(thunks):
        out.append(thunk())
        if i % LEVEL_GROUP == LEVEL_GROUP - 1 or i == len(thunks) - 1:
            yield
    return out


def _unit_lower_inverse(mats):
    c = mats[0].shape[0]
    row = lax.broadcasted_iota(jnp.int32, (c, c), 0)
    col = lax.broadcasted_iota(jnp.int32, (c, c), 1)
    apart = row ^ col
    eye = jnp.where(row == col, 1.0, 0.0)
    diag = [jnp.where(apart < INV_BASE, a, 0.0) for a in mats]
    ts = [eye - a for a in diag]
    apows = yield from _level([lambda a=a: _dot(a.astype(_bf16), a.astype(_bf16)) for a in diag])
    n = 2
    while n < INV_BASE:
        apbs = [ap.astype(_bf16) for ap in apows]
        if 2 * n < INV_BASE:
            both = yield from _level(
                [lambda t=t, apb=apb: _dot(jnp.concatenate([t.astype(_bf16), apb], axis=0), apb)
                 for t, apb in zip(ts, apbs)])
            ts = [t + b[:c] for t, b in zip(ts, both)]
            apows = [b[c:] for b in both]
        else:
            ts = yield from _level([lambda t=t, apb=apb: t + _dot(t.astype(_bf16), apb)
                                    for t, apb in zip(ts, apbs)])
        n *= 2
    size = INV_BASE
    while size < c:
        lower_left = jnp.logical_and(apart >= size, apart < 2 * size)
        tbs = [t.astype(_bf16) for t in ts]
        xs = yield from _level([lambda a=a, tb=tb: _dot(jnp.where(lower_left, a, 0.0).astype(_bf16), tb)
                                for a, tb in zip(mats, tbs)])
        ts = yield from _level([lambda t=t, tb=tb, x=x: t - _dot(tb, x.astype(_bf16))
                                for t, tb, x in zip(ts, tbs, xs)])
        size *= 2
    return ts


def _delta_chunks(probs, states):
    c = DELTA_CHUNK
    row = lax.broadcasted_iota(jnp.int32, (c, c), 0)
    col = lax.broadcasted_iota(jnp.int32, (c, c), 1)
    qn, kn, v, beta, gc = (list(t) for t in zip(*probs))
    n = len(probs)
    n_heads = len(states)
    rng = range(n)
    decay = [jnp.exp(jnp.where(row >= col, gc[i] - gc[i].T, -jnp.inf)) for i in rng]
    qs = [qn[i] * (HEAD_DIM ** -0.5) for i in rng]
    kb = [kn[i].astype(_bf16) for i in rng]
    qk_kk = yield from _level(
        [lambda i=i: _dot_nt(jnp.concatenate([qs[i].astype(_bf16), kb[i]], axis=0), kb[i]) for i in rng])
    qk = [(qk_kk[i][:c] * decay[i]).astype(_bf16) for i in rng]
    a_mat = [jnp.where(row > col, qk_kk[i][c:] * beta[i] * decay[i], 0.0) for i in rng]
    t_mat = yield from _unit_lower_inverse(a_mat)
    eg = [jnp.exp(gc[i]) for i in rng]
    rhs = [jnp.concatenate([v[i] * beta[i], kn[i] * (beta[i] * eg[i])], axis=1).astype(_bf16)
           for i in rng]
    uw = yield from _level([lambda i=i: _dot(t_mat[i].astype(_bf16), rhs[i]) for i in rng])
    g_last = [gc[i][c - 1:c] for i in rng]
    k_dec_t = [(kn[i] * jnp.exp(g_last[i] - gc[i])).T.astype(_bf16) for i in rng]
    wq = [jnp.concatenate([uw[i][:, HEAD_DIM:], qs[i] * eg[i]], axis=0).astype(_bf16) for i in rng]
    outs = []
    for first in range(0, n, n_heads):
        heads = range(n_heads)
        ws_qs = yield from _level(
            [lambda h=h: _dot(wq[first + h], states[h].astype(_bf16)) for h in heads])
        vb = [(uw[first + h][:, :HEAD_DIM] - ws_qs[h][:c]).astype(_bf16) for h in heads]
        outs += yield from _level(
            [lambda h=h: ws_qs[h][c:] + _dot(qk[first + h], vb[h]) for h in heads])
        states = yield from _level(
            [lambda h=h: states[h] * jnp.exp(g_last[first + h]) + _dot(k_dec_t[first + h], vb[h])
             for h in heads])
    return outs, states


def _spread(n_items, n_slots):
    return [(i + 1) * n_items // n_slots - i * n_items // n_slots for i in range(n_slots)]


def _run_interleaved(gen, fillers=(), plan=()):
    fillers = list(fillers)
    for count in list(plan) + [0] * 256:
        try:
            next(gen)
        except StopIteration as stop:
            for f in fillers:
                f()
            return stop.value
        for _ in range(min(count, len(fillers))):
            fillers.pop(0)()
    raise AssertionError("generator yielded more often than planned for")


def _merge_pieces(first, p_ref, pprev_ref, pmeta_ref, sa_ref, sb_ref, o_gated, x_ref,
                  poolw_ref, pscale_ref, wpa_ref, wpb_ref, wout_ref, nw_ref, out_ref, hn_ref):
    d = x_ref.shape[1]
    unit = MERGE_UNIT_COLS
    cols = range(0, d, unit)
    val = {}

    def pool():
        prev = jnp.where(first, pmeta_ref[...], pprev_ref[...])
        p = p_ref[...]
        ext = jnp.concatenate([prev, p], axis=0)
        gd = POOL_GROUP_DIM
        zb = []
        for gi, win in enumerate(POOL_WINDOWS):
            acc = ext[:, gi * gd:(gi + 1) * gd]
            s = 1
            while s < win:
                acc = acc + pltpu.roll(acc, s, axis=0)
                s *= 2
            pooled = acc[N_META:] * (1.0 / win) - p[:, gi * gd:(gi + 1) * gd]
            zb.append(_dot(pooled.astype(_bf16), poolw_ref[gi]))
        val["zb"] = (jnp.concatenate(zb, axis=1) * pscale_ref[...]).astype(_bf16)

    def branch(name, lhs, w_ref, gate_ref, c0):
        def run():
            y = _dot(lhs(), w_ref[:, c0:c0 + unit])
            val[name, c0] = gate_ref[:, c0:c0 + unit].astype(_f32) * y
        return run

    def out_cols(c0):
        def run():
            if "merged" not in val:
                merged = [val["a", m0] + val["b", m0] for m0 in cols]
                val["merged"] = jnp.concatenate(merged, axis=1).astype(_bf16)
            val["h", c0] = x_ref[:, c0:c0 + unit] + _dot(val["merged"], wout_ref[:, c0:c0 + unit])
        return run

    def finish():
        h1 = jnp.concatenate([val["h", c0] for c0 in cols], axis=1)
        out_ref[...] = h1
        hn_ref[...] = _rms_norm(h1, nw_ref[...]).astype(hn_ref.dtype)

    branches = ([branch("b", lambda: val["zb"], wpb_ref, sb_ref, c0) for c0 in cols]
                + [branch("a", o_gated, wpa_ref, sa_ref, c0) for c0 in cols])
    return [pool] + branches, [out_cols(c0) for c0 in cols] + [finish]


def _mixer_kernel(nblk, q_ref, k_ref, v_ref, zs_ref, ba_ref, mq_ref, mk_ref, mv_ref, mba_ref,
                  cwq_ref, cwk_ref, cwv_ref, alog_ref, dtb_ref, onw_ref,
                  p_ref, pprev_ref, pmeta_ref, sa_ref, sb_ref, x_ref,
                  poolw_ref, pscale_ref, wpa_ref, wpb_ref, wout_ref, nw_ref,
                  out_ref, hn_ref, state_ref, hq_ref, hk_ref, hv_ref, obuf_ref):
    c = DELTA_CHUNK
    hd = HEAD_DIM
    heads = range(N_HEADS)
    rows = q_ref.shape[0]
    s = pl.program_id(1)

    def delta_step(q_raw, k_raw, v_raw, ba, hq, hk, hv, valid, states):
        qc = _conv_silu(q_raw, hq, cwq_ref[...])
        yield
        kc = _conv_silu(k_raw, hk, cwk_ref[...])
        yield
        vc = _conv_silu(v_raw, hv, cwv_ref[...])
        yield
        beta_all = _sigmoid(ba)
        g_all = -jnp.exp(alog_ref[...]) * _softplus(ba + dtb_ref[...])
        if valid is not None:
            beta_all = jnp.where(valid, beta_all, 0.0)
            g_all = jnp.where(valid, g_all, 0.0)
        gc_all = _chunk_cumsum(g_all)
        probs = []
        for r0 in range(0, ba.shape[0], c):
            rs = slice(r0, r0 + c)
            for h in heads:
                sl = slice(h * hd, (h + 1) * hd)
                probs.append((_l2_norm(qc[rs, sl]), _l2_norm(kc[rs, sl]), vc[rs, sl],
                              _lane_column(beta_all[rs], h), _lane_column(gc_all[rs], N_HEADS + h)))
        yield
        return (yield from _delta_chunks(probs, states))

    @pl.when(s == 0)
    def _():
        mq, mk, mv = mq_ref[...], mk_ref[...], mv_ref[...]
        zeros = jnp.zeros((HIST_ROWS, N_HEADS * hd), _f32)
        is_meta = lax.broadcasted_iota(jnp.int32, (c, 128), 0) >= c - N_META
        _, states = _run_interleaved(delta_step(
            mq, mk, mv, mba_ref[...], zeros, zeros, zeros, is_meta,
            [jnp.zeros((hd, hd), _f32) for _ in heads]))
        for h in heads:
            state_ref[h] = states[h]
        hq_ref[...] = mq[c - HIST_ROWS:]
        hk_ref[...] = mk[c - HIST_ROWS:]
        hv_ref[...] = mv[c - HIST_ROWS:]
        obuf_ref[...] = jnp.zeros_like(obuf_ref)

    q_raw, k_raw, v_raw = q_ref[...], k_ref[...], v_ref[...]
    slot = s % 2
    early, late = _merge_pieces(s <= 1, p_ref, pprev_ref, pmeta_ref, sa_ref, sb_ref,
                                lambda: obuf_ref[1 - slot],
                                x_ref, poolw_ref, pscale_ref, wpa_ref, wpb_ref, wout_ref, nw_ref,
                                out_ref, hn_ref)
    base, extra = divmod(len(early), 4)
    n_level_yields = 15 * (rows // c) * N_HEADS // LEVEL_GROUP
    outs, states = _run_interleaved(
        delta_step(q_raw, k_raw, v_raw, ba_ref[...], hq_ref[...], hk_ref[...], hv_ref[...], None,
                   [state_ref[h] for h in heads]),
        early + late,
        plan=[base + 1] * extra + [base] * (4 - extra) + _spread(len(late), n_level_yields))
    hq_ref[...] = q_raw[rows - HIST_ROWS:]
    hk_ref[...] = k_raw[rows - HIST_ROWS:]
    hv_ref[...] = v_raw[rows - HIST_ROWS:]
    for h in heads:
        state_ref[h] = states[h]
    blocks = []
    for ci in range(rows // c):
        gated = []
        for h in heads:
            o = outs[ci * N_HEADS + h]
            o = o * lax.rsqrt(jnp.mean(o * o, axis=-1, keepdims=True) + RMS_EPS)
            sl = slice(h * hd, (h + 1) * hd)
            z = zs_ref[ci * c:(ci + 1) * c, sl].astype(_f32)
            gated.append((o * onw_ref[...] * z).astype(obuf_ref.dtype))
        blocks.append(jnp.concatenate(gated, axis=1))
    obuf_ref[slot] = jnp.concatenate(blocks, axis=0)


def _mixer(lin, act, ba, meta_lin, meta_lin_pad, meta_ba_pad, conv_w, alog_row, dtb_row, onw_row, x,
           pool_w, pool_scale, w_proj_a, w_proj_b, w_out, ffn_norm_w, batch, p_part, zs_part):
    m, d = x.shape
    c = DELTA_CHUNK
    rows = MIXER_ROWS
    assert (m // batch) % rows == 0 and rows % c == 0
    nblk = m // batch // rows
    hd = HEAD_DIM
    w = N_HEADS * hd
    pd = pool_w.shape[0] * POOL_GROUP_DIM
    assert pd == w
    sub = rows // N_META

    def delta_row(b, s):
        return b * nblk + jnp.minimum(s, nblk - 1)

    def merge_row(b, s):
        return b * nblk + jnp.maximum(s - 1, 0)

    def tok(part):
        return pl.BlockSpec((rows, w), lambda b, s: (delta_row(b, s), part))

    def meta(part):
        return pl.BlockSpec((c, w), lambda b, s: (0, part))

    def cw(part):
        return pl.BlockSpec((CONV_K, w), lambda b, s: (0, part))

    def const(shape):
        return pl.BlockSpec(shape, lambda b, s: (0,) * len(shape), pipeline_mode=pl.Buffered(1))

    row128 = pl.BlockSpec((1, 128), lambda b, s: (0, 0))
    return pl.pallas_call(
        functools.partial(_mixer_kernel, nblk),
        grid=(batch, nblk + 1),
        in_specs=[
            tok(0), tok(1), tok(2), tok(zs_part),
            pl.BlockSpec((rows, 128), lambda b, s: (delta_row(b, s), 0)),
            meta(0), meta(1), meta(2),
            pl.BlockSpec((c, 128), lambda b, s: (0, 0)),
            cw(0), cw(1), cw(2),
            row128, row128, row128,
            pl.BlockSpec((rows, pd), lambda b, s: (merge_row(b, s), p_part)),
            pl.BlockSpec((N_META, pd), lambda b, s: (jnp.maximum(merge_row(b, s) * sub - 1, 0), p_part)),
            pl.BlockSpec((N_META, pd), lambda b, s: (0, p_part)),
            pl.BlockSpec((rows, d), lambda b, s: (merge_row(b, s), 0)),
            pl.BlockSpec((rows, d), lambda b, s: (merge_row(b, s), 1)),
            pl.BlockSpec((rows, d), lambda b, s: (merge_row(b, s), 0)),
            const(pool_w.shape), const((1, pd)), const(w_proj_a.shape), const(w_proj_b.shape),
            const(w_out.shape), const((1, d)),
        ],
        out_specs=[pl.BlockSpec((rows, d), lambda b, s: (merge_row(b, s), 0)),
                   pl.BlockSpec((rows, d), lambda b, s: (merge_row(b, s), 0))],
        out_shape=[jax.ShapeDtypeStruct((m, d), _f32), jax.ShapeDtypeStruct((m, d), _bf16)],
        scratch_shapes=[
            pltpu.VMEM((N_HEADS, hd, hd), _f32),
            pltpu.VMEM((HIST_ROWS, w), _f32),
            pltpu.VMEM((HIST_ROWS, w), _f32),
            pltpu.VMEM((HIST_ROWS, w), _f32),
            pltpu.VMEM((2, rows, w), _bf16),
        ],
        compiler_params=pltpu.CompilerParams(
            dimension_semantics=("arbitrary", "arbitrary"),
            vmem_limit_bytes=_V7X_VMEM_LIMIT),
        name="mixer",
    )(lin, lin, lin, act, ba, meta_lin_pad, meta_lin_pad, meta_lin_pad, meta_ba_pad,
      conv_w, conv_w, conv_w, alog_row, dtb_row, onw_row,
      lin, lin, meta_lin, act, act, x,
      pool_w, pool_scale, w_proj_a, w_proj_b, w_out, ffn_norm_w)


def _ffn_kernel(h_hbm, hn_ref, wg_ref, wu_ref, wd_ref, fw_ref, out_ref, h_buf, h_sem):
    i = pl.program_id(0)
    f = pl.program_id(1)
    tm = out_ref.shape[0]

    def residual_copy():
        rows = pl.ds(pl.multiple_of(i * tm, tm), tm)
        return pltpu.make_async_copy(h_hbm.at[rows], h_buf, h_sem)

    @pl.when(f == 0)
    def _():
        residual_copy().start()
        out_ref[...] = jnp.zeros_like(out_ref)

    d = out_ref.shape[1]
    rc, dc = min(tm, 512), min(d, 512)
    for r0 in range(0, tm, rc):
        hn = hn_ref[r0:r0 + rc, :]
        act = (_silu(_dot(hn, wg_ref[...])) * _dot(hn, wu_ref[...])).astype(_bf16)
        for c0 in range(0, d, dc):
            out_ref[r0:r0 + rc, c0:c0 + dc] += _dot(act, wd_ref[:, c0:c0 + dc])

    @pl.when(f == pl.num_programs(1) - 1)
    def _():
        residual_copy().wait()
        out_ref[...] = _rms_norm(h_buf[...] + out_ref[...], fw_ref[...])


def _ffn(h, hn, w_gate, w_up, w_down, final_w, tm, tf):
    m, d = h.shape
    f = w_gate.shape[1]
    return pl.pallas_call(
        _ffn_kernel,
        grid=(m // tm, f // tf),
        in_specs=[
            pl.BlockSpec(memory_space=pl.ANY),
            pl.BlockSpec((tm, d), lambda i, j: (i, 0)),
            pl.BlockSpec((d, tf), lambda i, j: (0, j)),
            pl.BlockSpec((d, tf), lambda i, j: (0, j)),
            pl.BlockSpec((tf, d), lambda i, j: (j, 0)),
            pl.BlockSpec((1, d), lambda i, j: (0, 0)),
        ],
        out_specs=pl.BlockSpec((tm, d), lambda i, j: (i, 0)),
        out_shape=jax.ShapeDtypeStruct((m, d), _f32),
        scratch_shapes=[pltpu.VMEM((tm, d), _f32), pltpu.SemaphoreType.DMA(())],
        compiler_params=pltpu.CompilerParams(
            dimension_semantics=("arbitrary", "arbitrary"),
            vmem_limit_bytes=_V7X_VMEM_LIMIT),
        name="ffn",
    )(h, hn, w_gate, w_up, w_down, final_w)


def _largest_tile(n, cap):
    t = cap
    while n % t:
        t //= 2
    return t


def kernel(x, meta_tokens, norm_mix_w, w_in, conv_w, a_log, dt_bias, o_norm_w, w_proj_a, pool_w,
           pool_scale, w_proj_b, w_out, norm_ffn_w, w_ffn_gate, w_ffn_up, w_ffn_down, norm_final_w):
    assert w_in.shape[0] == 1, "single layer block"
    batch, seq, d = x.shape
    h, hd = N_HEADS, HEAD_DIM
    qk = h * hd
    pd = pool_w.shape[1] * POOL_GROUP_DIM
    assert seq % DELTA_CHUNK == 0 and N_META <= DELTA_CHUNK
    m = batch * seq

    w_t = w_in[0].T
    off_z = 3 * qk
    off_ba = 4 * qk
    off_p = off_ba + 2 * h
    off_g = off_p + pd
    tn = qk
    lin_cols, sig_cols = off_z + pd, 2 * d
    assert pd == qk and sig_cols % qk == 0
    row_offsets = ([0, qk, 2 * qk, off_p] + [off_g + c0 for c0 in range(0, sig_cols, tn)] + [off_z])
    w_main_t = _w_prep(w_t, row_offsets, tn, _largest_tile(d, 1024))
    w_ba_t = jnp.pad(w_t[off_ba:off_p], ((0, 128 - 2 * h), (0, 0))).astype(_bf16)
    p_part, zs_part = off_z // pd, sig_cols // qk

    x2 = x.reshape(m, d)
    nw = norm_mix_w[0].reshape(1, d)
    meta_lin, meta_ba = _in_proj(meta_tokens, nw, w_main_t, w_ba_t, N_META, tn, lin_cols, sig_cols,
                                 lin_only=True)
    lin, act, ba, wg_bf, wu_bf, wd_bf, wpa_bf, wpb_bf, wout_bf = _in_proj(
        x2, nw, w_main_t, w_ba_t, _largest_tile(m, 1024), tn, lin_cols, sig_cols,
        casts=(w_ffn_gate[0], w_ffn_up[0], w_ffn_down[0], w_proj_a[0], w_proj_b[0], w_out[0]))
    pad = ((DELTA_CHUNK - N_META, 0), (0, 0))
    meta_lin_pad = jnp.pad(meta_lin, pad)
    meta_ba_pad = jnp.pad(meta_ba, pad)

    lane_pad = lambda a: jnp.pad(a.reshape(1, h), ((0, 0), (h, 128 - 2 * h)))
    h1, hn2 = _mixer(lin, act, ba, meta_lin, meta_lin_pad, meta_ba_pad, conv_w[0], lane_pad(a_log[0]),
                     lane_pad(dt_bias[0]), o_norm_w[0].reshape(1, hd), x2, pool_w[0].astype(_bf16),
                     pool_scale[0].reshape(1, pd), wpa_bf, wpb_bf, wout_bf,
                     norm_ffn_w[0].reshape(1, d), batch, p_part, zs_part)

    out = _ffn(h1, hn2, wg_bf, wu_bf, wd_bf, norm_final_w.reshape(1, d), _largest_tile(m, 1024),
               _largest_tile(w_ffn_gate.shape[2], 512))
    return out.reshape(batch, seq, d)
```

```python
import functools

import jax
import jax.numpy as jnp
from jax import lax
from jax.experimental import pallas as pl
from jax.experimental.pallas import tpu as pltpu

N_META = 16
N_HEADS = 8
HEAD_DIM = 128
CONV_K = 4
POOL_WINDOWS = (2, 4, 8, 16)
POOL_GROUP_DIM = 256
RMS_EPS = 1e-6
L2_EPS = 1e-6

DELTA_CHUNK = 128
INV_BASE = 16
HIST_ROWS = 8
MIXER_ROWS = 256

_V7X_VMEM_LIMIT = 56 * 1024 * 1024

_f32 = jnp.float32
_bf16 = jnp.bfloat16


def _dot(a, b):
    return jnp.dot(a, b, preferred_element_type=_f32)


def _dot_nt(a, b):
    return lax.dot_general(a, b, (((1,), (1,)), ((), ())), preferred_element_type=_f32)


def _sigmoid(x):
    return 1.0 / (1.0 + jnp.exp(-x))


def _silu(x):
    return x * _sigmoid(x)


def _softplus(x):
    return jnp.maximum(x, 0.0) + jnp.log1p(jnp.exp(-jnp.abs(x)))


def _rms_norm(x, w):
    return x * lax.rsqrt(jnp.mean(x * x, axis=-1, keepdims=True) + RMS_EPS) * w


def _w_prep_kernel(w_ref, o_ref):
    o_ref[...] = w_ref[...].astype(o_ref.dtype)


def _w_prep(w_t, row_offsets, tn, tk):
    n_src, k = w_t.shape
    assert all(off % 8 == 0 and off + tn <= n_src for off in row_offsets)

    def row_off(j):
        tile = 0
        for jj, o in enumerate(row_offsets):
            tile = jnp.where(j == jj, o // 8, tile)
        return tile * 8

    return pl.pallas_call(
        _w_prep_kernel,
        grid=(len(row_offsets), k // tk),
        in_specs=[pl.BlockSpec((pl.Element(tn), pl.Element(tk)),
                               lambda j, c: (row_off(j), c * tk))],
        out_specs=pl.BlockSpec((tn, tk), lambda j, c: (j, c)),
        out_shape=jax.ShapeDtypeStruct((len(row_offsets) * tn, k), _bf16),
        compiler_params=pltpu.CompilerParams(
            dimension_semantics=("arbitrary", "arbitrary"),
            vmem_limit_bytes=_V7X_VMEM_LIMIT),
        name="w_prep",
    )(w_t)


CAST_COL_SPLIT = 4


def _sigmoid_tanh(x):
    return 0.5 * jnp.tanh(0.5 * x) + 0.5


def _in_proj_kernel(n_lin, n_sig, lin_only, cast_starts, x_ref, nw_ref, wba_ref, w_ref, *refs):
    n_cast = len(cast_starts)
    cast_src, refs = refs[:n_cast], list(refs[n_cast:])
    lin_ref = refs.pop(0)
    act_ref = None if lin_only else refs.pop(0)
    ba_ref = refs.pop(0)
    cast_dst, hn_ref = refs[:n_cast], refs[n_cast]
    j = pl.program_id(1)
    tm = x_ref.shape[0]
    rc = min(tm, 512)

    windows = [(start, start + CAST_COL_SPLIT) for start in cast_starts]
    branch_windows = [(0, n_lin), (n_lin, n_lin + n_sig)]

    def side_casts(window):
        for win, src, dst in zip(windows, cast_src, cast_dst):
            if win == window:
                dst[...] = src[...].astype(dst.dtype)

    @pl.when(j == 0)
    def _():
        for r0 in range(0, tm, rc):
            rows = slice(r0, r0 + rc)
            hn = _rms_norm(x_ref[rows, :], nw_ref[...]).astype(_bf16)
            hn_ref[rows, :] = hn
            ba_ref[rows, :] = _dot_nt(hn, wba_ref[...])
            lin_ref[rows, :] = _dot_nt(hn, w_ref[...])
        side_casts(branch_windows[0])

    @pl.when(jnp.logical_and(j > 0, j < n_lin))
    def _():
        lin_ref[...] = _dot_nt(hn_ref[...], w_ref[...])
        side_casts(branch_windows[0])

    if not lin_only:
        @pl.when(jnp.logical_and(j >= n_lin, j < n_lin + n_sig))
        def _():
            act_ref[...] = _sigmoid_tanh(_dot_nt(hn_ref[...], w_ref[...])).astype(act_ref.dtype)
            side_casts(branch_windows[1])

        @pl.when(j >= n_lin + n_sig)
        def _():
            y = _dot_nt(hn_ref[...], w_ref[...])
            act_ref[...] = (y * _sigmoid_tanh(y)).astype(act_ref.dtype)

    for win, src, dst in zip(windows, cast_src, cast_dst):
        if win not in branch_windows:
            @pl.when(jnp.logical_and(j >= win[0], j < win[1]))
            def _(src=src, dst=dst):
                dst[...] = src[...].astype(dst.dtype)


def _in_proj(x, norm_w, w_main_t, w_ba_t, tm, tn, lin_cols, sig_cols, casts=(), lin_only=False):
    m, d = x.shape
    n = lin_cols if lin_only else w_main_t.shape[0]
    assert lin_cols % tn == 0 and sig_cols % tn == 0 and n % tn == 0
    n_lin, n_sig = lin_cols // tn, sig_cols // tn
    gi, gj = m // tm, n // tn
    assert gj >= CAST_COL_SPLIT or not casts
    if n_lin == CAST_COL_SPLIT and n_sig == CAST_COL_SPLIT:
        load = {0: 0, n_lin: 0}
        cast_starts = []
        for a in casts:
            start = min(load, key=load.get)
            cast_starts.append(start)
            load[start] += a.size
        cast_starts = tuple(cast_starts)
    else:
        cast_starts = tuple(min(CAST_COL_SPLIT * k, gj - CAST_COL_SPLIT) for k in range(len(casts)))
    act_spec, act_shape = [], []
    if not lin_only:
        act_spec = [pl.BlockSpec((tm, tn), lambda i, j: (i, jnp.maximum(j - n_lin, 0)))]
        act_shape = [jax.ShapeDtypeStruct((m, n - lin_cols), _bf16)]
    cast_specs = []
    for start, a in zip(cast_starts, casts):
        rows, cols = a.shape
        assert rows % (16 * gi) == 0 and cols % (128 * CAST_COL_SPLIT) == 0
        cast_specs.append(pl.BlockSpec(
            (rows // gi, cols // CAST_COL_SPLIT),
            lambda i, j, start=start: (i, jnp.clip(j - start, 0, CAST_COL_SPLIT - 1))))
    return pl.pallas_call(
        functools.partial(_in_proj_kernel, n_lin, n_sig, lin_only, cast_starts),
        grid=(gi, gj),
        in_specs=[
            pl.BlockSpec((tm, d), lambda i, j: (i, 0)),
            pl.BlockSpec((1, d), lambda i, j: (0, 0)),
            pl.BlockSpec((128, d), lambda i, j: (0, 0)),
            pl.BlockSpec((tn, d), lambda i, j: (j, 0)),
        ] + cast_specs,
        out_specs=[
            pl.BlockSpec((tm, tn), lambda i, j: (i, jnp.minimum(j, n_lin - 1))),
        ] + act_spec + [
            pl.BlockSpec((tm, 128), lambda i, j: (i, 0)),
        ] + cast_specs,
        out_shape=[jax.ShapeDtypeStruct((m, lin_cols), _f32)] + act_shape + [
            jax.ShapeDtypeStruct((m, 128), _f32),
        ] + [jax.ShapeDtypeStruct(a.shape, _bf16) for a in casts],
        scratch_shapes=[pltpu.VMEM((tm, d), _bf16)],
        compiler_params=pltpu.CompilerParams(
            dimension_semantics=("arbitrary", "arbitrary"),
            vmem_limit_bytes=_V7X_VMEM_LIMIT),
        name="in_proj",
    )(x, norm_w, w_ba_t, w_main_t, *casts)


def _conv_silu(x, hist, cw):
    assert CONV_K == 4
    ext = jnp.concatenate([hist, x], axis=0)
    ext1 = pltpu.roll(ext, 1, axis=0)
    old = ext * cw[1:2] + ext1 * cw[0:1]
    acc = ext * cw[3:4] + ext1 * cw[2:3] + pltpu.roll(old, 2, axis=0)
    return _silu(acc[HIST_ROWS:])


def _l2_norm(x):
    return x * lax.rsqrt(jnp.sum(x * x, axis=-1, keepdims=True) + L2_EPS)


def _chunk_cumsum(g):
    pos = lax.broadcasted_iota(jnp.int32, g.shape, 0) % DELTA_CHUNK
    s = 1
    while s < DELTA_CHUNK:
        g = g + jnp.where(pos >= s, pltpu.roll(g, s, axis=0), 0.0)
        s *= 2
    return g


def _lane_column(x, lane_idx):
    lane = lax.broadcasted_iota(jnp.int32, x.shape, 1)
    col = jnp.sum(jnp.where(lane == lane_idx, x, 0.0), axis=1, keepdims=True)
    return jnp.broadcast_to(col, x.shape)


LEVEL_GROUP = 4
MERGE_UNIT_COLS = 256


def _level(thunks):
    out = []
    for i, thunk in enumerate(thunks):
        out.append(thunk())
        if i % LEVEL_GROUP == LEVEL_GROUP - 1 or i == len(thunks) - 1:
            yield
    return out


def _unit_lower_inverse(mats):
    c = mats[0].shape[0]
    row = lax.broadcasted_iota(jnp.int32, (c, c), 0)
    col = lax.broadcasted_iota(jnp.int32, (c, c), 1)
    apart = row ^ col
    eye = jnp.where(row == col, 1.0, 0.0)
    diag = [jnp.where(apart < INV_BASE, a, 0.0) for a in mats]
    ts = [eye - a for a in diag]
    apows = yield from _level([lambda a=a: _dot(a.astype(_bf16), a.astype(_bf16)) for a in diag])
    n = 2
    while n < INV_BASE:
        apbs = [ap.astype(_bf16) for ap in apows]
        if 2 * n < INV_BASE:
            both = yield from _level(
                [lambda t=t, apb=apb: _dot(jnp.concatenate([t.astype(_bf16), apb], axis=0), apb)
                 for t, apb in zip(ts, apbs)])
            ts = [t + b[:c] for t, b in zip(ts, both)]
            apows = [b[c:] for b in both]
        else:
            ts = yield from _level([lambda t=t, apb=apb: t + _dot(t.astype(_bf16), apb)
                                    for t, apb in zip(ts, apbs)])
        n *= 2
    size = INV_BASE
    while size < c:
        lower_left = jnp.logical_and(apart >= size, apart < 2 * size)
        tbs = [t.astype(_bf16) for t in ts]
        xs = yield from _level([lambda a=a, tb=tb: _dot(jnp.where(lower_left, a, 0.0).astype(_bf16), tb)
                                for a, tb in zip(mats, tbs)])
        ts = yield from _level([lambda t=t, tb=tb, x=x: t - _dot(tb, x.astype(_bf16))
                                for t, tb, x in zip(ts, tbs, xs)])
        size *= 2
    return ts


def _delta_chunks(probs, states):
    c = DELTA_CHUNK
    row = lax.broadcasted_iota(jnp.int32, (c, c), 0)
    col = lax.broadcasted_iota(jnp.int32, (c, c), 1)
    qn, kn, v, beta, gc = (list(t) for t in zip(*probs))
    n = len(probs)
    n_heads = len(states)
    rng = range(n)
    decay = [jnp.exp(jnp.where(row >= col, gc[i] - gc[i].T, -jnp.inf)) for i in rng]
    qs = [qn[i] * (HEAD_DIM ** -0.5) for i in rng]
    kb = [kn[i].astype(_bf16) for i in rng]
    qk_kk = yield from _level(
        [lambda i=i: _dot_nt(jnp.concatenate([qs[i].astype(_bf16), kb[i]], axis=0), kb[i]) for i in rng])
    qk = [(qk_kk[i][:c] * decay[i]).astype(_bf16) for i in rng]
    a_mat = [jnp.where(row > col, qk_kk[i][c:] * beta[i] * decay[i], 0.0) for i in rng]
    t_mat = yield from _unit_lower_inverse(a_mat)
    eg = [jnp.exp(gc[i]) for i in rng]
    rhs = [jnp.concatenate([v[i] * beta[i], kn[i] * (beta[i] * eg[i])], axis=1).astype(_bf16)
           for i in rng]
    uw = yield from _level([lambda i=i: _dot(t_mat[i].astype(_bf16), rhs[i]) for i in rng])
    g_last = [gc[i][c - 1:c] for i in rng]
    k_dec_t = [(kn[i] * jnp.exp(g_last[i] - gc[i])).T.astype(_bf16) for i in rng]
    wq = [jnp.concatenate([uw[i][:, HEAD_DIM:], qs[i] * eg[i]], axis=0).astype(_bf16) for i in rng]
    outs = []
    for first in range(0, n, n_heads):
        heads = range(n_heads)
        ws_qs = yield from _level(
            [lambda h=h: _dot(wq[first + h], states[h].astype(_bf16)) for h in heads])
        vb = [(uw[first + h][:, :HEAD_DIM] - ws_qs[h][:c]).astype(_bf16) for h in heads]
        outs += yield from _level(
            [lambda h=h: ws_qs[h][c:] + _dot(qk[first + h], vb[h]) for h in heads])
        states = yield from _level(
            [lambda h=h: states[h] * jnp.exp(g_last[first + h]) + _dot(k_dec_t[first + h], vb[h])
             for h in heads])
    return outs, states


def _spread(n_items, n_slots):
    return [(i + 1) * n_items // n_slots - i * n_items // n_slots for i in range(n_slots)]


def _run_interleaved(gen, fillers=(), plan=()):
    fillers = list(fillers)
    for count in list(plan) + [0] * 256:
        try:
            next(gen)
        except StopIteration as stop:
            for f in fillers:
                f()
            return stop.value
        for _ in range(min(count, len(fillers))):
            fillers.pop(0)()
    raise AssertionError("generator yielded more often than planned for")


def _merge_pieces(first, p_ref, pprev_ref, pmeta_ref, sa_ref, sb_ref, o_gated, x_ref,
                  poolw_ref, pscale_ref, wpa_ref, wpb_ref, wout_ref, nw_ref, out_ref, hn_ref):
    d = x_ref.shape[1]
    unit = MERGE_UNIT_COLS
    cols = range(0, d, unit)
    val = {}

    def pool():
        prev = jnp.where(first, pmeta_ref[...], pprev_ref[...])
        p = p_ref[...]
        ext = jnp.concatenate([prev, p], axis=0)
        gd = POOL_GROUP_DIM
        zb = []
        for gi, win in enumerate(POOL_WINDOWS):
            acc = ext[:, gi * gd:(gi + 1) * gd]
            s = 1
            while s < win:
                acc = acc + pltpu.roll(acc, s, axis=0)
                s *= 2
            pooled = acc[N_META:] * (1.0 / win) - p[:, gi * gd:(gi + 1) * gd]
            zb.append(_dot(pooled.astype(_bf16), poolw_ref[gi]))
        val["zb"] = (jnp.concatenate(zb, axis=1) * pscale_ref[...]).astype(_bf16)

    def branch(name, lhs, w_ref, gate_ref, c0):
        def run():
            y = _dot(lhs(), w_ref[:, c0:c0 + unit])
            val[name, c0] = gate_ref[:, c0:c0 + unit].astype(_f32) * y
        return run

    def out_cols(c0):
        def run():
            if "merged" not in val:
                merged = [val["a", m0] + val["b", m0] for m0 in cols]
                val["merged"] = jnp.concatenate(merged, axis=1).astype(_bf16)
            val["h", c0] = x_ref[:, c0:c0 + unit] + _dot(val["merged"], wout_ref[:, c0:c0 + unit])
        return run

    def finish():
        h1 = jnp.concatenate([val["h", c0] for c0 in cols], axis=1)
        out_ref[...] = h1
        hn_ref[...] = _rms_norm(h1, nw_ref[...]).astype(hn_ref.dtype)

    branches = ([branch("b", lambda: val["zb"], wpb_ref, sb_ref, c0) for c0 in cols]
                + [branch("a", o_gated, wpa_ref, sa_ref, c0) for c0 in cols])
    return [pool] + branches, [out_cols(c0) for c0 in cols] + [finish]


def _mixer_kernel(nblk, q_ref, k_ref, v_ref, zs_ref, ba_ref, mq_ref, mk_ref, mv_ref, mba_ref,
                  cwq_ref, cwk_ref, cwv_ref, alog_ref, dtb_ref, onw_ref,
                  p_ref, pprev_ref, pmeta_ref, sa_ref, sb_ref, x_ref,
                  poolw_ref, pscale_ref, wpa_ref, wpb_ref, wout_ref, nw_ref,
                  out_ref, hn_ref, state_ref, hq_ref, hk_ref, hv_ref, obuf_ref):
    c = DELTA_CHUNK
    hd = HEAD_DIM
    heads = range(N_HEADS)
    rows = q_ref.shape[0]
    s = pl.program_id(1)

    def delta_step(q_raw, k_raw, v_raw, ba, hq, hk, hv, valid, states):
        qc = _conv_silu(q_raw, hq, cwq_ref[...])
        yield
        kc = _conv_silu(k_raw, hk, cwk_ref[...])
        yield
        vc = _conv_silu(v_raw, hv, cwv_ref[...])
        yield
        beta_all = _sigmoid(ba)
        g_all = -jnp.exp(alog_ref[...]) * _softplus(ba + dtb_ref[...])
        if valid is not None:
            beta_all = jnp.where(valid, beta_all, 0.0)
            g_all = jnp.where(valid, g_all, 0.0)
        gc_all = _chunk_cumsum(g_all)
        probs = []
        for r0 in range(0, ba.shape[0], c):
            rs = slice(r0, r0 + c)
            for h in heads:
                sl = slice(h * hd, (h + 1) * hd)
                probs.append((_l2_norm(qc[rs, sl]), _l2_norm(kc[rs, sl]), vc[rs, sl],
                              _lane_column(beta_all[rs], h), _lane_column(gc_all[rs], N_HEADS + h)))
        yield
        return (yield from _delta_chunks(probs, states))

    @pl.when(s == 0)
    def _():
        mq, mk, mv = mq_ref[...], mk_ref[...], mv_ref[...]
        zeros = jnp.zeros((HIST_ROWS, N_HEADS * hd), _f32)
        is_meta = lax.broadcasted_iota(jnp.int32, (c, 128), 0) >= c - N_META
        _, states = _run_interleaved(delta_step(
            mq, mk, mv, mba_ref[...], zeros, zeros, zeros, is_meta,
            [jnp.zeros((hd, hd), _f32) for _ in heads]))
        for h in heads:
            state_ref[h] = states[h]
        hq_ref[...] = mq[c - HIST_ROWS:]
        hk_ref[...] = mk[c - HIST_ROWS:]
        hv_ref[...] = mv[c - HIST_ROWS:]
        obuf_ref[...] = jnp.zeros_like(obuf_ref)

    q_raw, k_raw, v_raw = q_ref[...], k_ref[...], v_ref[...]
    slot = s % 2
    early, late = _merge_pieces(s <= 1, p_ref, pprev_ref, pmeta_ref, sa_ref, sb_ref,
                                lambda: obuf_ref[1 - slot],
                                x_ref, poolw_ref, pscale_ref, wpa_ref, wpb_ref, wout_ref, nw_ref,
                                out_ref, hn_ref)
    base, extra = divmod(len(early), 4)
    n_level_yields = 15 * (rows // c) * N_HEADS // LEVEL_GROUP
    outs, states = _run_interleaved(
        delta_step(q_raw, k_raw, v_raw, ba_ref[...], hq_ref[...], hk_ref[...], hv_ref[...], None,
                   [state_ref[h] for h in heads]),
        early + late,
        plan=[base + 1] * extra + [base] * (4 - extra) + _spread(len(late), n_level_yields))
    hq_ref[...] = q_raw[rows - HIST_ROWS:]
    hk_ref[...] = k_raw[rows - HIST_ROWS:]
    hv_ref[...] = v_raw[rows - HIST_ROWS:]
    for h in heads:
        state_ref[h] = states[h]
    blocks = []
    for ci in range(rows // c):
        gated = []
        for h in heads:
            o = outs[ci * N_HEADS + h]
            o = o * lax.rsqrt(jnp.mean(o * o, axis=-1, keepdims=True) + RMS_EPS)
            sl = slice(h * hd, (h + 1) * hd)
            z = zs_ref[ci * c:(ci + 1) * c, sl].astype(_f32)
            gated.append((o * onw_ref[...] * z).astype(obuf_ref.dtype))
        blocks.append(jnp.concatenate(gated, axis=1))
    obuf_ref[slot] = jnp.concatenate(blocks, axis=0)


def _mixer(lin, act, ba, meta_lin, meta_lin_pad, meta_ba_pad, conv_w, alog_row, dtb_row, onw_row, x,
           pool_w, pool_scale, w_proj_a, w_proj_b, w_out, ffn_norm_w, batch, p_part, zs_part):
    m, d = x.shape
    c = DELTA_CHUNK
    rows = MIXER_ROWS
    assert (m // batch) % rows == 0 and rows % c == 0
    nblk = m // batch // rows
    hd = HEAD_DIM
    w = N_HEADS * hd
    pd = pool_w.shape[0] * POOL_GROUP_DIM
    assert pd == w
    sub = rows // N_META

    def delta_row(b, s):
        return b * nblk + jnp.minimum(s, nblk - 1)

    def merge_row(b, s):
        return b * nblk + jnp.maximum(s - 1, 0)

    def tok(part):
        return pl.BlockSpec((rows, w), lambda b, s: (delta_row(b, s), part))

    def meta(part):
        return pl.BlockSpec((c, w), lambda b, s: (0, part))

    def cw(part):
        return pl.BlockSpec((CONV_K, w), lambda b, s: (0, part))

    def const(shape):
        return pl.BlockSpec(shape, lambda b, s: (0,) * len(shape), pipeline_mode=pl.Buffered(1))

    row128 = pl.BlockSpec((1, 128), lambda b, s: (0, 0))
    return pl.pallas_call(
        functools.partial(_mixer_kernel, nblk),
        grid=(batch, nblk + 1),
        in_specs=[
            tok(0), tok(1), tok(2), tok(zs_part),
            pl.BlockSpec((rows, 128), lambda b, s: (delta_row(b, s), 0)),
            meta(0), meta(1), meta(2),
            pl.BlockSpec((c, 128), lambda b, s: (0, 0)),
            cw(0), cw(1), cw(2),
            row128, row128, row128,
            pl.BlockSpec((rows, pd), lambda b, s: (merge_row(b, s), p_part)),
            pl.BlockSpec((N_META, pd), lambda b, s: (jnp.maximum(merge_row(b, s) * sub - 1, 0), p_part)),
            pl.BlockSpec((N_META, pd), lambda b, s: (0, p_part)),
            pl.BlockSpec((rows, d), lambda b, s: (merge_row(b, s), 0)),
            pl.BlockSpec((rows, d), lambda b, s: (merge_row(b, s), 1)),
            pl.BlockSpec((rows, d), lambda b, s: (merge_row(b, s), 0)),
            const(pool_w.shape), const((1, pd)), const(w_proj_a.shape), const(w_proj_b.shape),
            const(w_out.shape), const((1, d)),
        ],
        out_specs=[pl.BlockSpec((rows, d), lambda b, s: (merge_row(b, s), 0)),
                   pl.BlockSpec((rows, d), lambda b, s: (merge_row(b, s), 0))],
        out_shape=[jax.ShapeDtypeStruct((m, d), _f32), jax.ShapeDtypeStruct((m, d), _bf16)],
        scratch_shapes=[
            pltpu.VMEM((N_HEADS, hd, hd), _f32),
            pltpu.VMEM((HIST_ROWS, w), _f32),
            pltpu.VMEM((HIST_ROWS, w), _f32),
            pltpu.VMEM((HIST_ROWS, w), _f32),
            pltpu.VMEM((2, rows, w), _bf16),
        ],
        compiler_params=pltpu.CompilerParams(
            dimension_semantics=("arbitrary", "arbitrary"),
            vmem_limit_bytes=_V7X_VMEM_LIMIT),
        name="mixer",
    )(lin, lin, lin, act, ba, meta_lin_pad, meta_lin_pad, meta_lin_pad, meta_ba_pad,
      conv_w, conv_w, conv_w, alog_row, dtb_row, onw_row,
      lin, lin, meta_lin, act, act, x,
      pool_w, pool_scale, w_proj_a, w_proj_b, w_out, ffn_norm_w)


def _ffn_kernel(h_hbm, hn_ref, wg_ref, wu_ref, wd_ref, fw_ref, out_ref, h_buf, h_sem):
    i = pl.program_id(0)
    f = pl.program_id(1)
    tm = out_ref.shape[0]

    def residual_copy():
        rows = pl.ds(pl.multiple_of(i * tm, tm), tm)
        return pltpu.make_async_copy(h_hbm.at[rows], h_buf, h_sem)

    @pl.when(f == 0)
    def _():
        residual_copy().start()
        out_ref[...] = jnp.zeros_like(out_ref)

    d = out_ref.shape[1]
    rc, dc = min(tm, 512), min(d, 512)

    def accumulate(r0):
        hn = hn_ref[r0:r0 + rc, :]
        act = (_silu(_dot(hn, wg_ref[...])) * _dot(hn, wu_ref[...])).astype(_bf16)
        for c0 in range(0, d, dc):
            out_ref[r0:r0 + rc, c0:c0 + dc] += _dot(act, wd_ref[:, c0:c0 + dc])

    last = pl.num_programs(1) - 1

    @pl.when(f < last)
    def _():
        for r0 in range(0, tm, rc):
            accumulate(r0)

    @pl.when(f == last)
    def _():
        residual_copy().wait()
        for r0 in range(0, tm, rc):
            accumulate(r0)
            rows = slice(r0, r0 + rc)
            out_ref[rows, :] = _rms_norm(h_buf[rows, :] + out_ref[rows, :], fw_ref[...])


def _ffn(h, hn, w_gate, w_up, w_down, final_w, tm, tf):
    m, d = h.shape
    f = w_gate.shape[1]
    return pl.pallas_call(
        _ffn_kernel,
        grid=(m // tm, f // tf),
        in_specs=[
            pl.BlockSpec(memory_space=pl.ANY),
            pl.BlockSpec((tm, d), lambda i, j: (i, 0)),
            pl.BlockSpec((d, tf), lambda i, j: (0, j)),
            pl.BlockSpec((d, tf), lambda i, j: (0, j)),
            pl.BlockSpec((tf, d), lambda i, j: (j, 0)),
            pl.BlockSpec((1, d), lambda i, j: (0, 0)),
        ],
        out_specs=pl.BlockSpec((tm, d), lambda i, j: (i, 0)),
        out_shape=jax.ShapeDtypeStruct((m, d), _f32),
        scratch_shapes=[pltpu.VMEM((tm, d), _f32), pltpu.SemaphoreType.DMA(())],
        compiler_params=pltpu.CompilerParams(
            dimension_semantics=("arbitrary", "arbitrary"),
            vmem_limit_bytes=_V7X_VMEM_LIMIT),
        name="ffn",
    )(h, hn, w_gate, w_up, w_down, final_w)


def _largest_tile(n, cap):
    t = cap
    while n % t:
        t //= 2
    return t


def kernel(x, meta_tokens, norm_mix_w, w_in, conv_w, a_log, dt_bias, o_norm_w, w_proj_a, pool_w,
           pool_scale, w_proj_b, w_out, norm_ffn_w, w_ffn_gate, w_ffn_up, w_ffn_down, norm_final_w):
    assert w_in.shape[0] == 1, "single layer block"
    batch, seq, d = x.shape
    h, hd = N_HEADS, HEAD_DIM
    qk = h * hd
    pd = pool_w.shape[1] * POOL_GROUP_DIM
    assert seq % DELTA_CHUNK == 0 and N_META <= DELTA_CHUNK
    m = batch * seq

    w_t = w_in[0].T
    off_z = 3 * qk
    off_ba = 4 * qk
    off_p = off_ba + 2 * h
    off_g = off_p + pd
    tn = qk
    lin_cols, sig_cols = off_z + pd, 2 * d
    assert pd == qk and sig_cols % qk == 0
    row_offsets = ([0, qk, 2 * qk, off_p] + [off_g + c0 for c0 in range(0, sig_cols, tn)] + [off_z])
    w_main_t = _w_prep(w_t, row_offsets, tn, _largest_tile(d, 1024))
    w_ba_t = jnp.pad(w_t[off_ba:off_p], ((0, 128 - 2 * h), (0, 0))).astype(_bf16)
    p_part, zs_part = off_z // pd, sig_cols // qk

    x2 = x.reshape(m, d)
    nw = norm_mix_w[0].reshape(1, d)
    meta_lin, meta_ba = _in_proj(meta_tokens, nw, w_main_t, w_ba_t, N_META, tn, lin_cols, sig_cols,
                                 lin_only=True)
    lin, act, ba, wg_bf, wu_bf, wd_bf, wpa_bf, wpb_bf, wout_bf = _in_proj(
        x2, nw, w_main_t, w_ba_t, _largest_tile(m, 1024), tn, lin_cols, sig_cols,
        casts=(w_ffn_gate[0], w_ffn_up[0], w_ffn_down[0], w_proj_a[0], w_proj_b[0], w_out[0]))
    pad = ((DELTA_CHUNK - N_META, 0), (0, 0))
    meta_lin_pad = jnp.pad(meta_lin, pad)
    meta_ba_pad = jnp.pad(meta_ba, pad)

    lane_pad = lambda a: jnp.pad(a.reshape(1, h), ((0, 0), (h, 128 - 2 * h)))
    h1, hn2 = _mixer(lin, act, ba, meta_lin, meta_lin_pad, meta_ba_pad, conv_w[0], lane_pad(a_log[0]),
                     lane_pad(dt_bias[0]), o_norm_w[0].reshape(1, hd), x2, pool_w[0].astype(_bf16),
                     pool_scale[0].reshape(1, pd), wpa_bf, wpb_bf, wout_bf,
                     norm_ffn_w[0].reshape(1, d), batch, p_part, zs_part)

    out = _ffn(h1, hn2, wg_bf, wu_bf, wd_bf, norm_final_w.reshape(1, d), _largest_tile(m, 1024),
               _largest_tile(w_ffn_gate.shape[2], 512))
    return out.reshape(batch, seq, d)
```

```python
import functools

import jax
import jax.numpy as jnp
from jax import lax
from jax.experimental import pallas as pl
from jax.experimental.pallas import tpu as pltpu

N_META = 16
N_HEADS = 8
HEAD_DIM = 128
CONV_K = 4
POOL_WINDOWS = (2, 4, 8, 16)
POOL_GROUP_DIM = 256
RMS_EPS = 1e-6
L2_EPS = 1e-6

DELTA_CHUNK = 128
INV_BASE = 16
HIST_ROWS = 8
MIXER_ROWS = 256

_V7X_VMEM_LIMIT = 56 * 1024 * 1024

_f32 = jnp.float32
_bf16 = jnp.bfloat16


def _dot(a, b):
    return jnp.dot(a, b, preferred_element_type=_f32)


def _dot_nt(a, b):
    return lax.dot_general(a, b, (((1,), (1,)), ((), ())), preferred_element_type=_f32)


def _sigmoid(x):
    return 1.0 / (1.0 + jnp.exp(-x))


def _silu(x):
    return x * _sigmoid(x)


def _softplus(x):
    return jnp.maximum(x, 0.0) + jnp.log1p(jnp.exp(-jnp.abs(x)))


def _rms_norm(x, w):
    return x * lax.rsqrt(jnp.mean(x * x, axis=-1, keepdims=True) + RMS_EPS) * w


def _w_prep_kernel(w_ref, o_ref):
    o_ref[...] = w_ref[...].astype(o_ref.dtype)


def _w_prep(w_t, row_offsets, tn, tk):
    n_src, k = w_t.shape
    assert all(off % 8 == 0 and off + tn <= n_src for off in row_offsets)

    def row_off(j):
        tile = 0
        for jj, o in enumerate(row_offsets):
            tile = jnp.where(j == jj, o // 8, tile)
        return tile * 8

    return pl.pallas_call(
        _w_prep_kernel,
        grid=(len(row_offsets), k // tk),
        in_specs=[pl.BlockSpec((pl.Element(tn), pl.Element(tk)),
                               lambda j, c: (row_off(j), c * tk))],
        out_specs=pl.BlockSpec((tn, tk), lambda j, c: (j, c)),
        out_shape=jax.ShapeDtypeStruct((len(row_offsets) * tn, k), _bf16),
        compiler_params=pltpu.CompilerParams(
            dimension_semantics=("arbitrary", "arbitrary"),
            vmem_limit_bytes=_V7X_VMEM_LIMIT),
        name="w_prep",
    )(w_t)


CAST_COL_SPLIT = 4
NORM_OVERLAP_ROWS = 256


def _sigmoid_tanh(x):
    return 0.5 * jnp.tanh(0.5 * x) + 0.5


def _in_proj_kernel(n_lin, n_sig, lin_only, cast_starts, x_ref, nw_ref, wba_ref, w_ref, *refs):
    n_cast = len(cast_starts)
    cast_src, refs = refs[:n_cast], list(refs[n_cast:])
    lin_ref = refs.pop(0)
    act_ref = None if lin_only else refs.pop(0)
    ba_ref = refs.pop(0)
    cast_dst, hn_ref = refs[:n_cast], refs[n_cast]
    j = pl.program_id(1)
    tm = x_ref.shape[0]
    rc = min(tm, NORM_OVERLAP_ROWS)

    windows = [(start, start + CAST_COL_SPLIT) for start in cast_starts]
    branch_windows = [(0, n_lin), (n_lin, n_lin + n_sig)]

    def side_casts(window):
        for win, src, dst in zip(windows, cast_src, cast_dst):
            if win == window:
                dst[...] = src[...].astype(dst.dtype)

    @pl.when(j == 0)
    def _():
        for r0 in range(0, tm, rc):
            rows = slice(r0, r0 + rc)
            hn = _rms_norm(x_ref[rows, :], nw_ref[...]).astype(_bf16)
            hn_ref[rows, :] = hn
            ba_ref[rows, :] = _dot_nt(hn, wba_ref[...])
            lin_ref[rows, :] = _dot_nt(hn, w_ref[...])
        side_casts(branch_windows[0])

    @pl.when(jnp.logical_and(j > 0, j < n_lin))
    def _():
        lin_ref[...] = _dot_nt(hn_ref[...], w_ref[...])
        side_casts(branch_windows[0])

    if not lin_only:
        @pl.when(jnp.logical_and(j >= n_lin, j < n_lin + n_sig))
        def _():
            act_ref[...] = _sigmoid_tanh(_dot_nt(hn_ref[...], w_ref[...])).astype(act_ref.dtype)
            side_casts(branch_windows[1])

        @pl.when(j >= n_lin + n_sig)
        def _():
            y = _dot_nt(hn_ref[...], w_ref[...])
            act_ref[...] = (y * _sigmoid_tanh(y)).astype(act_ref.dtype)

    for win, src, dst in zip(windows, cast_src, cast_dst):
        if win not in branch_windows:
            @pl.when(jnp.logical_and(j >= win[0], j < win[1]))
            def _(src=src, dst=dst):
                dst[...] = src[...].astype(dst.dtype)


def _in_proj(x, norm_w, w_main_t, w_ba_t, tm, tn, lin_cols, sig_cols, casts=(), lin_only=False):
    m, d = x.shape
    n = lin_cols if lin_only else w_main_t.shape[0]
    assert lin_cols % tn == 0 and sig_cols % tn == 0 and n % tn == 0
    n_lin, n_sig = lin_cols // tn, sig_cols // tn
    gi, gj = m // tm, n // tn
    assert gj >= CAST_COL_SPLIT or not casts
    if n_lin == CAST_COL_SPLIT and n_sig == CAST_COL_SPLIT:
        load = {0: 0, n_lin: 0}
        cast_starts = []
        for a in casts:
            start = min(load, key=load.get)
            cast_starts.append(start)
            load[start] += a.size
        cast_starts = tuple(cast_starts)
    else:
        cast_starts = tuple(min(CAST_COL_SPLIT * k, gj - CAST_COL_SPLIT) for k in range(len(casts)))
    act_spec, act_shape = [], []
    if not lin_only:
        act_spec = [pl.BlockSpec((tm, tn), lambda i, j: (i, jnp.maximum(j - n_lin, 0)))]
        act_shape = [jax.ShapeDtypeStruct((m, n - lin_cols), _bf16)]
    cast_specs = []
    for start, a in zip(cast_starts, casts):
        rows, cols = a.shape
        assert rows % (16 * gi) == 0 and cols % (128 * CAST_COL_SPLIT) == 0
        cast_specs.append(pl.BlockSpec(
            (rows // gi, cols // CAST_COL_SPLIT),
            lambda i, j, start=start: (i, jnp.clip(j - start, 0, CAST_COL_SPLIT - 1))))
    return pl.pallas_call(
        functools.partial(_in_proj_kernel, n_lin, n_sig, lin_only, cast_starts),
        grid=(gi, gj),
        in_specs=[
            pl.BlockSpec((tm, d), lambda i, j: (i, 0)),
            pl.BlockSpec((1, d), lambda i, j: (0, 0)),
            pl.BlockSpec((128, d), lambda i, j: (0, 0)),
            pl.BlockSpec((tn, d), lambda i, j: (j, 0)),
        ] + cast_specs,
        out_specs=[
            pl.BlockSpec((tm, tn), lambda i, j: (i, jnp.minimum(j, n_lin - 1))),
        ] + act_spec + [
            pl.BlockSpec((tm, 128), lambda i, j: (i, 0)),
        ] + cast_specs,
        out_shape=[jax.ShapeDtypeStruct((m, lin_cols), _f32)] + act_shape + [
            jax.ShapeDtypeStruct((m, 128), _f32),
        ] + [jax.ShapeDtypeStruct(a.shape, _bf16) for a in casts],
        scratch_shapes=[pltpu.VMEM((tm, d), _bf16)],
        compiler_params=pltpu.CompilerParams(
            dimension_semantics=("arbitrary", "arbitrary"),
            vmem_limit_bytes=_V7X_VMEM_LIMIT),
        name="in_proj",
    )(x, norm_w, w_ba_t, w_main_t, *casts)


def _conv_silu(x, hist, cw):
    assert CONV_K == 4
    ext = jnp.concatenate([hist, x], axis=0)
    ext1 = pltpu.roll(ext, 1, axis=0)
    old = ext * cw[1:2] + ext1 * cw[0:1]
    acc = ext * cw[3:4] + ext1 * cw[2:3] + pltpu.roll(old, 2, axis=0)
    return _silu(acc[HIST_ROWS:])


def _l2_norm(x):
    return x * lax.rsqrt(jnp.sum(x * x, axis=-1, keepdims=True) + L2_EPS)


def _chunk_cumsum(g):
    pos = lax.broadcasted_iota(jnp.int32, g.shape, 0) % DELTA_CHUNK
    s = 1
    while s < DELTA_CHUNK:
        g = g + jnp.where(pos >= s, pltpu.roll(g, s, axis=0), 0.0)
        s *= 2
    return g


def _lane_column(x, lane_idx):
    lane = lax.broadcasted_iota(jnp.int32, x.shape, 1)
    col = jnp.sum(jnp.where(lane == lane_idx, x, 0.0), axis=1, keepdims=True)
    return jnp.broadcast_to(col, x.shape)


LEVEL_GROUP = 4
MERGE_UNIT_COLS = 256


def _level(thunks):
    out = []
    for i, thunk in enumerate(thunks):
        out.append(thunk())
        if i % LEVEL_GROUP == LEVEL_GROUP - 1 or i == len(thunks) - 1:
            yield
    return out


def _unit_lower_inverse(mats):
    c = mats[0].shape[0]
    row = lax.broadcasted_iota(jnp.int32, (c, c), 0)
    col = lax.broadcasted_iota(jnp.int32, (c, c), 1)
    apart = row ^ col
    eye = jnp.where(row == col, 1.0, 0.0)
    diag = [jnp.where(apart < INV_BASE, a, 0.0) for a in mats]
    ts = [eye - a for a in diag]
    apows = yield from _level([lambda a=a: _dot(a.astype(_bf16), a.astype(_bf16)) for a in diag])
    n = 2
    while n < INV_BASE:
        apbs = [ap.astype(_bf16) for ap in apows]
        if 2 * n < INV_BASE:
            both = yield from _level(
                [lambda t=t, apb=apb: _dot(jnp.concatenate([t.astype(_bf16), apb], axis=0), apb)
                 for t, apb in zip(ts, apbs)])
            ts = [t + b[:c] for t, b in zip(ts, both)]
            apows = [b[c:] for b in both]
        else:
            ts = yield from _level([lambda t=t, apb=apb: t + _dot(t.astype(_bf16), apb)
                                    for t, apb in zip(ts, apbs)])
        n *= 2
    size = INV_BASE
    while size < c:
        lower_left = jnp.logical_and(apart >= size, apart < 2 * size)
        tbs = [t.astype(_bf16) for t in ts]
        xs = yield from _level([lambda a=a, tb=tb: _dot(jnp.where(lower_left, a, 0.0).astype(_bf16), tb)
                                for a, tb in zip(mats, tbs)])
        ts = yield from _level([lambda t=t, tb=tb, x=x: t - _dot(tb, x.astype(_bf16))
                                for t, tb, x in zip(ts, tbs, xs)])
        size *= 2
    return ts


def _delta_chunks(probs, states):
    c = DELTA_CHUNK
    row = lax.broadcasted_iota(jnp.int32, (c, c), 0)
    col = lax.broadcasted_iota(jnp.int32, (c, c), 1)
    qn, kn, v, beta, gc = (list(t) for t in zip(*probs))
    n = len(probs)
    n_heads = len(states)
    rng = range(n)
    decay = [jnp.exp(jnp.where(row >= col, gc[i] - gc[i].T, -jnp.inf)) for i in rng]
    qs = [qn[i] * (HEAD_DIM ** -0.5) for i in rng]
    kb = [kn[i].astype(_bf16) for i in rng]
    qk_kk = yield from _level(
        [lambda i=i: _dot_nt(jnp.concatenate([qs[i].astype(_bf16), kb[i]], axis=0), kb[i]) for i in rng])
    qk = [(qk_kk[i][:c] * decay[i]).astype(_bf16) for i in rng]
    a_mat = [jnp.where(row > col, qk_kk[i][c:] * beta[i] * decay[i], 0.0) for i in rng]
    t_mat = yield from _unit_lower_inverse(a_mat)
    eg = [jnp.exp(gc[i]) for i in rng]
    rhs = [jnp.concatenate([v[i] * beta[i], kn[i] * (beta[i] * eg[i])], axis=1).astype(_bf16)
           for i in rng]
    uw = yield from _level([lambda i=i: _dot(t_mat[i].astype(_bf16), rhs[i]) for i in rng])
    g_last = [gc[i][c - 1:c] for i in rng]
    k_dec_t = [(kn[i] * jnp.exp(g_last[i] - gc[i])).T.astype(_bf16) for i in rng]
    wq = [jnp.concatenate([uw[i][:, HEAD_DIM:], qs[i] * eg[i]], axis=0).astype(_bf16) for i in rng]
    outs = []
    for first in range(0, n, n_heads):
        heads = range(n_heads)
        ws_qs = yield from _level(
            [lambda h=h: _dot(wq[first + h], states[h].astype(_bf16)) for h in heads])
        vb = [(uw[first + h][:, :HEAD_DIM] - ws_qs[h][:c]).astype(_bf16) for h in heads]
        outs += yield from _level(
            [lambda h=h: ws_qs[h][c:] + _dot(qk[first + h], vb[h]) for h in heads])
        states = yield from _level(
            [lambda h=h: states[h] * jnp.exp(g_last[first + h]) + _dot(k_dec_t[first + h], vb[h])
             for h in heads])
    return outs, states


def _spread(n_items, n_slots):
    return [(i + 1) * n_items // n_slots - i * n_items // n_slots for i in range(n_slots)]


def _run_interleaved(gen, fillers=(), plan=()):
    fillers = list(fillers)
    for count in list(plan) + [0] * 256:
        try:
            next(gen)
        except StopIteration as stop:
            for f in fillers:
                f()
            return stop.value
        for _ in range(min(count, len(fillers))):
            fillers.pop(0)()
    raise AssertionError("generator yielded more often than planned for")


def _merge_pieces(first, p_ref, pprev_ref, pmeta_ref, sa_ref, sb_ref, o_gated, x_ref,
                  poolw_ref, pscale_ref, wpa_ref, wpb_ref, wout_ref, nw_ref, out_ref, hn_ref):
    d = x_ref.shape[1]
    unit = MERGE_UNIT_COLS
    cols = range(0, d, unit)
    val = {}

    def pool():
        prev = jnp.where(first, pmeta_ref[...], pprev_ref[...])
        p = p_ref[...]
        ext = jnp.concatenate([prev, p], axis=0)
        gd = POOL_GROUP_DIM
        zb = []
        for gi, win in enumerate(POOL_WINDOWS):
            acc = ext[:, gi * gd:(gi + 1) * gd]
            s = 1
            while s < win:
                acc = acc + pltpu.roll(acc, s, axis=0)
                s *= 2
            pooled = acc[N_META:] * (1.0 / win) - p[:, gi * gd:(gi + 1) * gd]
            zb.append(_dot(pooled.astype(_bf16), poolw_ref[gi]))
        val["zb"] = (jnp.concatenate(zb, axis=1) * pscale_ref[...]).astype(_bf16)

    def branch(name, lhs, w_ref, gate_ref, c0):
        def run():
            y = _dot(lhs(), w_ref[:, c0:c0 + unit])
            val[name, c0] = gate_ref[:, c0:c0 + unit].astype(_f32) * y
        return run

    def out_cols(c0):
        def run():
            if "merged" not in val:
                merged = [val["a", m0] + val["b", m0] for m0 in cols]
                val["merged"] = jnp.concatenate(merged, axis=1).astype(_bf16)
            val["h", c0] = x_ref[:, c0:c0 + unit] + _dot(val["merged"], wout_ref[:, c0:c0 + unit])
        return run

    def finish():
        h1 = jnp.concatenate([val["h", c0] for c0 in cols], axis=1)
        out_ref[...] = h1
        hn_ref[...] = _rms_norm(h1, nw_ref[...]).astype(hn_ref.dtype)

    branches = ([branch("b", lambda: val["zb"], wpb_ref, sb_ref, c0) for c0 in cols]
                + [branch("a", o_gated, wpa_ref, sa_ref, c0) for c0 in cols])
    return [pool] + branches, [out_cols(c0) for c0 in cols] + [finish]


def _mixer_kernel(nblk, q_ref, k_ref, v_ref, zs_ref, ba_ref, mq_ref, mk_ref, mv_ref, mba_ref,
                  cwq_ref, cwk_ref, cwv_ref, alog_ref, dtb_ref, onw_ref,
                  p_ref, pprev_ref, pmeta_ref, sa_ref, sb_ref, x_ref,
                  poolw_ref, pscale_ref, wpa_ref, wpb_ref, wout_ref, nw_ref,
                  out_ref, hn_ref, state_ref, hq_ref, hk_ref, hv_ref, obuf_ref):
    c = DELTA_CHUNK
    hd = HEAD_DIM
    heads = range(N_HEADS)
    rows = q_ref.shape[0]
    s = pl.program_id(1)

    def delta_step(q_raw, k_raw, v_raw, ba, hq, hk, hv, valid, states):
        qc = _conv_silu(q_raw, hq, cwq_ref[...])
        yield
        kc = _conv_silu(k_raw, hk, cwk_ref[...])
        yield
        vc = _conv_silu(v_raw, hv, cwv_ref[...])
        yield
        beta_all = _sigmoid(ba)
        g_all = -jnp.exp(alog_ref[...]) * _softplus(ba + dtb_ref[...])
        if valid is not None:
            beta_all = jnp.where(valid, beta_all, 0.0)
            g_all = jnp.where(valid, g_all, 0.0)
        gc_all = _chunk_cumsum(g_all)
        probs = []
        for r0 in range(0, ba.shape[0], c):
            rs = slice(r0, r0 + c)
            for h in heads:
                sl = slice(h * hd, (h + 1) * hd)
                probs.append((_l2_norm(qc[rs, sl]), _l2_norm(kc[rs, sl]), vc[rs, sl],
                              _lane_column(beta_all[rs], h), _lane_column(gc_all[rs], N_HEADS + h)))
        yield
        return (yield from _delta_chunks(probs, states))

    @pl.when(s == 0)
    def _():
        mq, mk, mv = mq_ref[...], mk_ref[...], mv_ref[...]
        zeros = jnp.zeros((HIST_ROWS, N_HEADS * hd), _f32)
        is_meta = lax.broadcasted_iota(jnp.int32, (c, 128), 0) >= c - N_META
        _, states = _run_interleaved(delta_step(
            mq, mk, mv, mba_ref[...], zeros, zeros, zeros, is_meta,
            [jnp.zeros((hd, hd), _f32) for _ in heads]))
        for h in heads:
            state_ref[h] = states[h]
        hq_ref[...] = mq[c - HIST_ROWS:]
        hk_ref[...] = mk[c - HIST_ROWS:]
        hv_ref[...] = mv[c - HIST_ROWS:]
        obuf_ref[...] = jnp.zeros_like(obuf_ref)

    q_raw, k_raw, v_raw = q_ref[...], k_ref[...], v_ref[...]
    slot = s % 2
    early, late = _merge_pieces(s <= 1, p_ref, pprev_ref, pmeta_ref, sa_ref, sb_ref,
                                lambda: obuf_ref[1 - slot],
                                x_ref, poolw_ref, pscale_ref, wpa_ref, wpb_ref, wout_ref, nw_ref,
                                out_ref, hn_ref)
    base, extra = divmod(len(early), 4)
    n_level_yields = 15 * (rows // c) * N_HEADS // LEVEL_GROUP
    outs, states = _run_interleaved(
        delta_step(q_raw, k_raw, v_raw, ba_ref[...], hq_ref[...], hk_ref[...], hv_ref[...], None,
                   [state_ref[h] for h in heads]),
        early + late,
        plan=[base + 1] * extra + [base] * (4 - extra) + _spread(len(late), n_level_yields))
    hq_ref[...] = q_raw[rows - HIST_ROWS:]
    hk_ref[...] = k_raw[rows - HIST_ROWS:]
    hv_ref[...] = v_raw[rows - HIST_ROWS:]
    for h in heads:
        state_ref[h] = states[h]
    blocks = []
    for ci in range(rows // c):
        gated = []
        for h in heads:
            o = outs[ci * N_HEADS + h]
            o = o * lax.rsqrt(jnp.mean(o * o, axis=-1, keepdims=True) + RMS_EPS)
            sl = slice(h * hd, (h + 1) * hd)
            z = zs_ref[ci * c:(ci + 1) * c, sl].astype(_f32)
            gated.append((o * onw_ref[...] * z).astype(obuf_ref.dtype))
        blocks.append(jnp.concatenate(gated, axis=1))
    obuf_ref[slot] = jnp.concatenate(blocks, axis=0)


def _mixer(lin, act, ba, meta_lin, meta_lin_pad, meta_ba_pad, conv_w, alog_row, dtb_row, onw_row, x,
           pool_w, pool_scale, w_proj_a, w_proj_b, w_out, ffn_norm_w, batch, p_part, zs_part):
    m, d = x.shape
    c = DELTA_CHUNK
    rows = MIXER_ROWS
    assert (m // batch) % rows == 0 and rows % c == 0
    nblk = m // batch // rows
    hd = HEAD_DIM
    w = N_HEADS * hd
    pd = pool_w.shape[0] * POOL_GROUP_DIM
    assert pd == w
    sub = rows // N_META

    def delta_row(b, s):
        return b * nblk + jnp.minimum(s, nblk - 1)

    def merge_row(b, s):
        return b * nblk + jnp.maximum(s - 1, 0)

    def tok(part):
        return pl.BlockSpec((rows, w), lambda b, s: (delta_row(b, s), part))

    def meta(part):
        return pl.BlockSpec((c, w), lambda b, s: (0, part))

    def cw(part):
        return pl.BlockSpec((CONV_K, w), lambda b, s: (0, part))

    def const(shape):
        return pl.BlockSpec(shape, lambda b, s: (0,) * len(shape), pipeline_mode=pl.Buffered(1))

    row128 = pl.BlockSpec((1, 128), lambda b, s: (0, 0))
    return pl.pallas_call(
        functools.partial(_mixer_kernel, nblk),
        grid=(batch, nblk + 1),
        in_specs=[
            tok(0), tok(1), tok(2), tok(zs_part),
            pl.BlockSpec((rows, 128), lambda b, s: (delta_row(b, s), 0)),
            meta(0), meta(1), meta(2),
            pl.BlockSpec((c, 128), lambda b, s: (0, 0)),
            cw(0), cw(1), cw(2),
            row128, row128, row128,
            pl.BlockSpec((rows, pd), lambda b, s: (merge_row(b, s), p_part)),
            pl.BlockSpec((N_META, pd), lambda b, s: (jnp.maximum(merge_row(b, s) * sub - 1, 0), p_part)),
            pl.BlockSpec((N_META, pd), lambda b, s: (0, p_part)),
            pl.BlockSpec((rows, d), lambda b, s: (merge_row(b, s), 0)),
            pl.BlockSpec((rows, d), lambda b, s: (merge_row(b, s), 1)),
            pl.BlockSpec((rows, d), lambda b, s: (merge_row(b, s), 0)),
            const(pool_w.shape), const((1, pd)), const(w_proj_a.shape), const(w_proj_b.shape),
            const(w_out.shape), const((1, d)),
        ],
        out_specs=[pl.BlockSpec((rows, d), lambda b, s: (merge_row(b, s), 0)),
                   pl.BlockSpec((rows, d), lambda b, s: (merge_row(b, s), 0))],
        out_shape=[jax.ShapeDtypeStruct((m, d), _f32), jax.ShapeDtypeStruct((m, d), _bf16)],
        scratch_shapes=[
            pltpu.VMEM((N_HEADS, hd, hd), _f32),
            pltpu.VMEM((HIST_ROWS, w), _f32),
            pltpu.VMEM((HIST_ROWS, w), _f32),
            pltpu.VMEM((HIST_ROWS, w), _f32),
            pltpu.VMEM((2, rows, w), _bf16),
        ],
        compiler_params=pltpu.CompilerParams(
            dimension_semantics=("arbitrary", "arbitrary"),
            vmem_limit_bytes=_V7X_VMEM_LIMIT),
        name="mixer",
    )(lin, lin, lin, act, ba, meta_lin_pad, meta_lin_pad, meta_lin_pad, meta_ba_pad,
      conv_w, conv_w, conv_w, alog_row, dtb_row, onw_row,
      lin, lin, meta_lin, act, act, x,
      pool_w, pool_scale, w_proj_a, w_proj_b, w_out, ffn_norm_w)


def _ffn_kernel(h_hbm, hn_ref, wg_ref, wu_ref, wd_ref, fw_ref, out_ref, h_buf, h_sem):
    i = pl.program_id(0)
    f = pl.program_id(1)
    tm = out_ref.shape[0]

    def residual_copy():
        rows = pl.ds(pl.multiple_of(i * tm, tm), tm)
        return pltpu.make_async_copy(h_hbm.at[rows], h_buf, h_sem)

    d = out_ref.shape[1]
    rc, dc = min(tm, 512), min(d, 512)

    def accumulate(r0, rows_per_chunk, first):
        rows = slice(r0, r0 + rows_per_chunk)
        hn = hn_ref[rows, :]
        act = (_silu(_dot(hn, wg_ref[...])) * _dot(hn, wu_ref[...])).astype(_bf16)
        for c0 in range(0, d, dc):
            part = _dot(act, wd_ref[:, c0:c0 + dc])
            if first:
                out_ref[rows, c0:c0 + dc] = part
            else:
                out_ref[rows, c0:c0 + dc] += part

    last = pl.num_programs(1) - 1

    @pl.when(f == 0)
    def _():
        residual_copy().start()
        for r0 in range(0, tm, rc):
            accumulate(r0, rc, True)

    @pl.when(jnp.logical_and(f > 0, f < last))
    def _():
        for r0 in range(0, tm, rc):
            accumulate(r0, rc, False)

    @pl.when(f == last)
    def _():
        residual_copy().wait()
        fc = min(tm, NORM_OVERLAP_ROWS)
        for r0 in range(0, tm, fc):
            accumulate(r0, fc, False)
            rows = slice(r0, r0 + fc)
            out_ref[rows, :] = _rms_norm(h_buf[rows, :] + out_ref[rows, :], fw_ref[...])


def _ffn(h, hn, w_gate, w_up, w_down, final_w, tm, tf):
    m, d = h.shape
    f = w_gate.shape[1]
    assert f // tf >= 2, "the first and the last ffn column step must be different steps"
    return pl.pallas_call(
        _ffn_kernel,
        grid=(m // tm, f // tf),
        in_specs=[
            pl.BlockSpec(memory_space=pl.ANY),
            pl.BlockSpec((tm, d), lambda i, j: (i, 0)),
            pl.BlockSpec((d, tf), lambda i, j: (0, j)),
            pl.BlockSpec((d, tf), lambda i, j: (0, j)),
            pl.BlockSpec((tf, d), lambda i, j: (j, 0)),
            pl.BlockSpec((1, d), lambda i, j: (0, 0)),
        ],
        out_specs=pl.BlockSpec((tm, d), lambda i, j: (i, 0)),
        out_shape=jax.ShapeDtypeStruct((m, d), _f32),
        scratch_shapes=[pltpu.VMEM((tm, d), _f32), pltpu.SemaphoreType.DMA(())],
        compiler_params=pltpu.CompilerParams(
            dimension_semantics=("arbitrary", "arbitrary"),
            vmem_limit_bytes=_V7X_VMEM_LIMIT),
        name="ffn",
    )(h, hn, w_gate, w_up, w_down, final_w)


def _largest_tile(n, cap):
    t = cap
    while n % t:
        t //= 2
    return t


def kernel(x, meta_tokens, norm_mix_w, w_in, conv_w, a_log, dt_bias, o_norm_w, w_proj_a, pool_w,
           pool_scale, w_proj_b, w_out, norm_ffn_w, w_ffn_gate, w_ffn_up, w_ffn_down, norm_final_w):
    assert w_in.shape[0] == 1, "single layer block"
    batch, seq, d = x.shape
    h, hd = N_HEADS, HEAD_DIM
    qk = h * hd
    pd = pool_w.shape[1] * POOL_GROUP_DIM
    assert seq % DELTA_CHUNK == 0 and N_META <= DELTA_CHUNK
    m = batch * seq

    w_t = w_in[0].T
    off_z = 3 * qk
    off_ba = 4 * qk
    off_p = off_ba + 2 * h
    off_g = off_p + pd
    tn = qk
    lin_cols, sig_cols = off_z + pd, 2 * d
    assert pd == qk and sig_cols % qk == 0
    row_offsets = ([0, qk, 2 * qk, off_p] + [off_g + c0 for c0 in range(0, sig_cols, tn)] + [off_z])
    w_main_t = _w_prep(w_t, row_offsets, tn, _largest_tile(d, 1024))
    w_ba_t = jnp.pad(w_t[off_ba:off_p], ((0, 128 - 2 * h), (0, 0))).astype(_bf16)
    p_part, zs_part = off_z // pd, sig_cols // qk

    x2 = x.reshape(m, d)
    nw = norm_mix_w[0].reshape(1, d)
    meta_lin, meta_ba = _in_proj(meta_tokens, nw, w_main_t, w_ba_t, N_META, tn, lin_cols, sig_cols,
                                 lin_only=True)
    lin, act, ba, wg_bf, wu_bf, wd_bf, wpa_bf, wpb_bf, wout_bf = _in_proj(
        x2, nw, w_main_t, w_ba_t, _largest_tile(m, 1024), tn, lin_cols, sig_cols,
        casts=(w_ffn_gate[0], w_ffn_up[0], w_ffn_down[0], w_proj_a[0], w_proj_b[0], w_out[0]))
    pad = ((DELTA_CHUNK - N_META, 0), (0, 0))
    meta_lin_pad = jnp.pad(meta_lin, pad)
    meta_ba_pad = jnp.pad(meta_ba, pad)

    lane_pad = lambda a: jnp.pad(a.reshape(1, h), ((0, 0), (h, 128 - 2 * h)))
    h1, hn2 = _mixer(lin, act, ba, meta_lin, meta_lin_pad, meta_ba_pad, conv_w[0], lane_pad(a_log[0]),
                     lane_pad(dt_bias[0]), o_norm_w[0].reshape(1, hd), x2, pool_w[0].astype(_bf16),
                     pool_scale[0].reshape(1, pd), wpa_bf, wpb_bf, wout_bf,
                     norm_ffn_w[0].reshape(1, d), batch, p_part, zs_part)

    out = _ffn(h1, hn2, wg_bf, wu_bf, wd_bf, norm_final_w.reshape(1, d), _largest_tile(m, 1024),
               _largest_tile(w_ffn_gate.shape[2], 512))
    return out.reshape(batch, seq, d)
```

```python
import functools

import jax
import jax.numpy as jnp
from jax import lax
from jax.experimental import pallas as pl
from jax.experimental.pallas import tpu as pltpu

N_META = 16
N_HEADS = 8
HEAD_DIM = 128
CONV_K = 4
POOL_WINDOWS = (2, 4, 8, 16)
POOL_GROUP_DIM = 256
RMS_EPS = 1e-6
L2_EPS = 1e-6

DELTA_CHUNK = 128
INV_BASE = 16
HIST_ROWS = 8
MIXER_ROWS = 256

_V7X_VMEM_LIMIT = 56 * 1024 * 1024

_f32 = jnp.float32
_bf16 = jnp.bfloat16


def _dot(a, b):
    return jnp.dot(a, b, preferred_element_type=_f32)


def _dot_nt(a, b):
    return lax.dot_general(a, b, (((1,), (1,)), ((), ())), preferred_element_type=_f32)


def _sigmoid(x):
    return 1.0 / (1.0 + jnp.exp(-x))


def _silu(x):
    return x * _sigmoid(x)


def _softplus(x):
    return jnp.maximum(x, 0.0) + jnp.log1p(jnp.exp(-jnp.abs(x)))


def _rms_norm(x, w):
    return x * lax.rsqrt(jnp.mean(x * x, axis=-1, keepdims=True) + RMS_EPS) * w


def _w_prep_kernel(w_ref, o_ref):
    o_ref[...] = w_ref[...].astype(o_ref.dtype)


def _w_prep(w_t, row_offsets, tn, tk):
    n_src, k = w_t.shape
    assert all(off % 8 == 0 and off + tn <= n_src for off in row_offsets)

    def row_off(j):
        tile = 0
        for jj, o in enumerate(row_offsets):
            tile = jnp.where(j == jj, o // 8, tile)
        return tile * 8

    return pl.pallas_call(
        _w_prep_kernel,
        grid=(len(row_offsets), k // tk),
        in_specs=[pl.BlockSpec((pl.Element(tn), pl.Element(tk)),
                               lambda j, c: (row_off(j), c * tk))],
        out_specs=pl.BlockSpec((tn, tk), lambda j, c: (j, c)),
        out_shape=jax.ShapeDtypeStruct((len(row_offsets) * tn, k), _bf16),
        compiler_params=pltpu.CompilerParams(
            dimension_semantics=("arbitrary", "arbitrary"),
            vmem_limit_bytes=_V7X_VMEM_LIMIT),
        name="w_prep",
    )(w_t)


CAST_COL_SPLIT = 4
NORM_OVERLAP_ROWS = 256


def _sigmoid_tanh(x):
    return 0.5 * jnp.tanh(0.5 * x) + 0.5


def _in_proj_kernel(n_lin, n_sig, lin_only, cast_starts, x_ref, nw_ref, wba_ref, w_ref, *refs):
    n_cast = len(cast_starts)
    cast_src, refs = refs[:n_cast], list(refs[n_cast:])
    lin_ref = refs.pop(0)
    act_ref = None if lin_only else refs.pop(0)
    ba_ref = refs.pop(0)
    cast_dst, hn_ref = refs[:n_cast], refs[n_cast]
    j = pl.program_id(1)
    tm = x_ref.shape[0]
    rc = min(tm, NORM_OVERLAP_ROWS)

    windows = [(start, start + CAST_COL_SPLIT) for start in cast_starts]
    branch_windows = [(0, n_lin), (n_lin, n_lin + n_sig)]

    def side_casts(window):
        for win, src, dst in zip(windows, cast_src, cast_dst):
            if win == window:
                dst[...] = src[...].astype(dst.dtype)

    @pl.when(j == 0)
    def _():
        for r0 in range(0, tm, rc):
            rows = slice(r0, r0 + rc)
            hn = _rms_norm(x_ref[rows, :], nw_ref[...]).astype(_bf16)
            hn_ref[rows, :] = hn
            ba_ref[rows, :] = _dot_nt(hn, wba_ref[...])
            lin_ref[rows, :] = _dot_nt(hn, w_ref[...])
        side_casts(branch_windows[0])

    @pl.when(jnp.logical_and(j > 0, j < n_lin))
    def _():
        lin_ref[...] = _dot_nt(hn_ref[...], w_ref[...])
        side_casts(branch_windows[0])

    if not lin_only:
        @pl.when(jnp.logical_and(j >= n_lin, j < n_lin + n_sig))
        def _():
            act_ref[...] = _sigmoid_tanh(_dot_nt(hn_ref[...], w_ref[...])).astype(act_ref.dtype)
            side_casts(branch_windows[1])

        @pl.when(j >= n_lin + n_sig)
        def _():
            y = _dot_nt(hn_ref[...], w_ref[...])
            act_ref[...] = (y * _sigmoid_tanh(y)).astype(act_ref.dtype)

    for win, src, dst in zip(windows, cast_src, cast_dst):
        if win not in branch_windows:
            @pl.when(jnp.logical_and(j >= win[0], j < win[1]))
            def _(src=src, dst=dst):
                dst[...] = src[...].astype(dst.dtype)


def _in_proj(x, norm_w, w_main_t, w_ba_t, tm, tn, lin_cols, sig_cols, casts=(), lin_only=False):
    m, d = x.shape
    n = lin_cols if lin_only else w_main_t.shape[0]
    assert lin_cols % tn == 0 and sig_cols % tn == 0 and n % tn == 0
    n_lin, n_sig = lin_cols // tn, sig_cols // tn
    gi, gj = m // tm, n // tn
    assert gj >= CAST_COL_SPLIT or not casts
    if n_lin == CAST_COL_SPLIT and n_sig == CAST_COL_SPLIT:
        load = {0: 0, n_lin: 0}
        cast_starts = []
        for a in casts:
            start = min(load, key=load.get)
            cast_starts.append(start)
            load[start] += a.size
        cast_starts = tuple(cast_starts)
    else:
        cast_starts = tuple(min(CAST_COL_SPLIT * k, gj - CAST_COL_SPLIT) for k in range(len(casts)))
    act_spec, act_shape = [], []
    if not lin_only:
        act_spec = [pl.BlockSpec((tm, tn), lambda i, j: (i, jnp.maximum(j - n_lin, 0)))]
        act_shape = [jax.ShapeDtypeStruct((m, n - lin_cols), _bf16)]
    cast_specs = []
    for start, a in zip(cast_starts, casts):
        rows, cols = a.shape
        assert rows % (16 * gi) == 0 and cols % (128 * CAST_COL_SPLIT) == 0
        cast_specs.append(pl.BlockSpec(
            (rows // gi, cols // CAST_COL_SPLIT),
            lambda i, j, start=start: (i, jnp.clip(j - start, 0, CAST_COL_SPLIT - 1))))
    return pl.pallas_call(
        functools.partial(_in_proj_kernel, n_lin, n_sig, lin_only, cast_starts),
        grid=(gi, gj),
        in_specs=[
            pl.BlockSpec((tm, d), lambda i, j: (i, 0)),
            pl.BlockSpec((1, d), lambda i, j: (0, 0)),
            pl.BlockSpec((128, d), lambda i, j: (0, 0)),
            pl.BlockSpec((tn, d), lambda i, j: (j, 0)),
        ] + cast_specs,
        out_specs=[
            pl.BlockSpec((tm, tn), lambda i, j: (i, jnp.minimum(j, n_lin - 1))),
        ] + act_spec + [
            pl.BlockSpec((tm, 128), lambda i, j: (i, 0)),
        ] + cast_specs,
        out_shape=[jax.ShapeDtypeStruct((m, lin_cols), _f32)] + act_shape + [
            jax.ShapeDtypeStruct((m, 128), _f32),
        ] + [jax.ShapeDtypeStruct(a.shape, _bf16) for a in casts],
        scratch_shapes=[pltpu.VMEM((tm, d), _bf16)],
        compiler_params=pltpu.CompilerParams(
            dimension_semantics=("arbitrary", "arbitrary"),
            vmem_limit_bytes=_V7X_VMEM_LIMIT),
        name="in_proj",
    )(x, norm_w, w_ba_t, w_main_t, *casts)


def _conv_silu(x, hist, cw):
    assert CONV_K == 4
    ext = jnp.concatenate([hist, x], axis=0)
    ext1 = pltpu.roll(ext, 1, axis=0)
    old = ext * cw[1:2] + ext1 * cw[0:1]
    acc = ext * cw[3:4] + ext1 * cw[2:3] + pltpu.roll(old, 2, axis=0)
    return _silu(acc[HIST_ROWS:])


def _l2_norm(x):
    return x * lax.rsqrt(jnp.sum(x * x, axis=-1, keepdims=True) + L2_EPS)


def _chunk_cumsum(g):
    pos = lax.broadcasted_iota(jnp.int32, g.shape, 0) % DELTA_CHUNK
    s = 1
    while s < DELTA_CHUNK:
        g = g + jnp.where(pos >= s, pltpu.roll(g, s, axis=0), 0.0)
        s *= 2
    return g


def _lane_column(x, lane_idx):
    lane = lax.broadcasted_iota(jnp.int32, x.shape, 1)
    col = jnp.sum(jnp.where(lane == lane_idx, x, 0.0), axis=1, keepdims=True)
    return jnp.broadcast_to(col, x.shape)


LEVEL_GROUP = 4
MERGE_UNIT_COLS = 256


def _level(thunks):
    out = []
    for i, thunk in enumerate(thunks):
        out.append(thunk())
        if i % LEVEL_GROUP == LEVEL_GROUP - 1 or i == len(thunks) - 1:
            yield
    return out


def _unit_lower_inverse(mats):
    c = mats[0].shape[0]
    row = lax.broadcasted_iota(jnp.int32, (c, c), 0)
    col = lax.broadcasted_iota(jnp.int32, (c, c), 1)
    apart = row ^ col
    eye = jnp.where(row == col, 1.0, 0.0)
    diag = [jnp.where(apart < INV_BASE, a, 0.0) for a in mats]
    ts = [eye - a for a in diag]
    apows = yield from _level([lambda a=a: _dot(a.astype(_bf16), a.astype(_bf16)) for a in diag])
    n = 2
    while n < INV_BASE:
        apbs = [ap.astype(_bf16) for ap in apows]
        if 2 * n < INV_BASE:
            both = yield from _level(
                [lambda t=t, apb=apb: _dot(jnp.concatenate([t.astype(_bf16), apb], axis=0), apb)
                 for t, apb in zip(ts, apbs)])
            ts = [t + b[:c] for t, b in zip(ts, both)]
            apows = [b[c:] for b in both]
        else:
            ts = yield from _level([lambda t=t, apb=apb: t + _dot(t.astype(_bf16), apb)
                                    for t, apb in zip(ts, apbs)])
        n *= 2
    size = INV_BASE
    while size < c:
        lower_left = jnp.logical_and(apart >= size, apart < 2 * size)
        tbs = [t.astype(_bf16) for t in ts]
        xs = yield from _level([lambda a=a, tb=tb: _dot(jnp.where(lower_left, a, 0.0).astype(_bf16), tb)
                                for a, tb in zip(mats, tbs)])
        ts = yield from _level([lambda t=t, tb=tb, x=x: t - _dot(tb, x.astype(_bf16))
                                for t, tb, x in zip(ts, tbs, xs)])
        size *= 2
    return ts


def _delta_chunks(probs, states):
    c = DELTA_CHUNK
    row = lax.broadcasted_iota(jnp.int32, (c, c), 0)
    col = lax.broadcasted_iota(jnp.int32, (c, c), 1)
    qn, kn, v, beta, gc = (list(t) for t in zip(*probs))
    n = len(probs)
    n_heads = len(states)
    rng = range(n)
    decay = [jnp.exp(jnp.where(row >= col, gc[i] - gc[i].T, -jnp.inf)) for i in rng]
    qs = [qn[i] * (HEAD_DIM ** -0.5) for i in rng]
    kb = [kn[i].astype(_bf16) for i in rng]
    qk_kk = yield from _level(
        [lambda i=i: _dot_nt(jnp.concatenate([qs[i].astype(_bf16), kb[i]], axis=0), kb[i]) for i in rng])
    qk = [(qk_kk[i][:c] * decay[i]).astype(_bf16) for i in rng]
    a_mat = [jnp.where(row > col, qk_kk[i][c:] * beta[i] * decay[i], 0.0) for i in rng]
    t_mat = yield from _unit_lower_inverse(a_mat)
    eg = [jnp.exp(gc[i]) for i in rng]
    rhs = [jnp.concatenate([v[i] * beta[i], kn[i] * (beta[i] * eg[i])], axis=1).astype(_bf16)
           for i in rng]
    uw = yield from _level([lambda i=i: _dot(t_mat[i].astype(_bf16), rhs[i]) for i in rng])
    g_last = [gc[i][c - 1:c] for i in rng]
    k_dec_t = [(kn[i] * jnp.exp(g_last[i] - gc[i])).T.astype(_bf16) for i in rng]
    wq = [jnp.concatenate([uw[i][:, HEAD_DIM:], qs[i] * eg[i]], axis=0).astype(_bf16) for i in rng]
    outs = []
    for first in range(0, n, n_heads):
        heads = range(n_heads)
        ws_qs = yield from _level(
            [lambda h=h: _dot(wq[first + h], states[h].astype(_bf16)) for h in heads])
        vb = [(uw[first + h][:, :HEAD_DIM] - ws_qs[h][:c]).astype(_bf16) for h in heads]
        outs += yield from _level(
            [lambda h=h: ws_qs[h][c:] + _dot(qk[first + h], vb[h]) for h in heads])
        states = yield from _level(
            [lambda h=h: states[h] * jnp.exp(g_last[first + h]) + _dot(k_dec_t[first + h], vb[h])
             for h in heads])
    return outs, states


def _spread(n_items, n_slots):
    return [(i + 1) * n_items // n_slots - i * n_items // n_slots for i in range(n_slots)]


def _run_interleaved(gen, fillers=(), plan=()):
    fillers = list(fillers)
    for count in list(plan) + [0] * 256:
        try:
            next(gen)
        except StopIteration as stop:
            for f in fillers:
                f()
            return stop.value
        for _ in range(min(count, len(fillers))):
            fillers.pop(0)()
    raise AssertionError("generator yielded more often than planned for")


def _merge_pieces(first, p_ref, pprev_ref, pmeta_ref, sa_ref, sb_ref, o_gated, x_ref,
                  poolw_ref, pscale_ref, wpa_ref, wpb_ref, wout_ref, nw_ref, out_ref, hn_ref):
    d = x_ref.shape[1]
    unit = MERGE_UNIT_COLS
    cols = range(0, d, unit)
    val = {}

    def pool():
        prev = jnp.where(first, pmeta_ref[...], pprev_ref[...])
        p = p_ref[...]
        ext = jnp.concatenate([prev, p], axis=0)
        gd = POOL_GROUP_DIM
        zb = []
        for gi, win in enumerate(POOL_WINDOWS):
            acc = ext[:, gi * gd:(gi + 1) * gd]
            s = 1
            while s < win:
                acc = acc + pltpu.roll(acc, s, axis=0)
                s *= 2
            pooled = acc[N_META:] * (1.0 / win) - p[:, gi * gd:(gi + 1) * gd]
            zb.append(_dot(pooled.astype(_bf16), poolw_ref[gi]))
        val["zb"] = (jnp.concatenate(zb, axis=1) * pscale_ref[...]).astype(_bf16)

    def branch(name, lhs, w_ref, gate_ref, c0):
        def run():
            y = _dot(lhs(), w_ref[:, c0:c0 + unit])
            val[name, c0] = gate_ref[:, c0:c0 + unit].astype(_f32) * y
        return run

    def out_cols(c0):
        def run():
            if "merged" not in val:
                merged = [val["a", m0] + val["b", m0] for m0 in cols]
                val["merged"] = jnp.concatenate(merged, axis=1).astype(_bf16)
            val["h", c0] = x_ref[:, c0:c0 + unit] + _dot(val["merged"], wout_ref[:, c0:c0 + unit])
        return run

    def finish():
        h1 = jnp.concatenate([val["h", c0] for c0 in cols], axis=1)
        out_ref[...] = h1
        hn_ref[...] = _rms_norm(h1, nw_ref[...]).astype(hn_ref.dtype)

    branches = ([branch("b", lambda: val["zb"], wpb_ref, sb_ref, c0) for c0 in cols]
                + [branch("a", o_gated, wpa_ref, sa_ref, c0) for c0 in cols])
    return [pool] + branches, [out_cols(c0) for c0 in cols] + [finish]


def _mixer_kernel(nblk, q_ref, k_ref, v_ref, zs_ref, ba_ref, mq_ref, mk_ref, mv_ref, mba_ref,
                  cwq_ref, cwk_ref, cwv_ref, alog_ref, dtb_ref, onw_ref,
                  p_ref, pprev_ref, pmeta_ref, sa_ref, sb_ref, x_ref,
                  poolw_ref, pscale_ref, wpa_ref, wpb_ref, wout_ref, nw_ref,
                  out_ref, hn_ref, state_ref, hq_ref, hk_ref, hv_ref, obuf_ref):
    c = DELTA_CHUNK
    hd = HEAD_DIM
    heads = range(N_HEADS)
    rows = q_ref.shape[0]
    s = pl.program_id(1)

    def delta_step(q_raw, k_raw, v_raw, ba, hq, hk, hv, valid, states):
        qc = _conv_silu(q_raw, hq, cwq_ref[...])
        yield
        kc = _conv_silu(k_raw, hk, cwk_ref[...])
        yield
        vc = _conv_silu(v_raw, hv, cwv_ref[...])
        yield
        beta_all = _sigmoid(ba)
        g_all = -jnp.exp(alog_ref[...]) * _softplus(ba + dtb_ref[...])
        if valid is not None:
            beta_all = jnp.where(valid, beta_all, 0.0)
            g_all = jnp.where(valid, g_all, 0.0)
        gc_all = _chunk_cumsum(g_all)
        probs = []
        for r0 in range(0, ba.shape[0], c):
            rs = slice(r0, r0 + c)
            for h in heads:
                sl = slice(h * hd, (h + 1) * hd)
                probs.append((_l2_norm(qc[rs, sl]), _l2_norm(kc[rs, sl]), vc[rs, sl],
                              _lane_column(beta_all[rs], h), _lane_column(gc_all[rs], N_HEADS + h)))
        yield
        return (yield from _delta_chunks(probs, states))

    def step(with_delta, with_merge):
        slot = s % 2
        pieces, plan = [], []
        if with_merge:
            early, late = _merge_pieces(s == 1, p_ref, pprev_ref, pmeta_ref, sa_ref, sb_ref,
                                        lambda: obuf_ref[1 - slot],
                                        x_ref, poolw_ref, pscale_ref, wpa_ref, wpb_ref, wout_ref,
                                        nw_ref, out_ref, hn_ref)
            base, extra = divmod(len(early), 4)
            n_level_yields = 15 * (rows // c) * N_HEADS // LEVEL_GROUP
            pieces = early + late
            plan = [base + 1] * extra + [base] * (4 - extra) + _spread(len(late), n_level_yields)
        if not with_delta:
            for piece in pieces:
                piece()
            return
        q_raw, k_raw, v_raw = q_ref[...], k_ref[...], v_ref[...]
        outs, states = _run_interleaved(
            delta_step(q_raw, k_raw, v_raw, ba_ref[...], hq_ref[...], hk_ref[...], hv_ref[...],
                       None, [state_ref[h] for h in heads]),
            pieces, plan)
        hq_ref[...] = q_raw[rows - HIST_ROWS:]
        hk_ref[...] = k_raw[rows - HIST_ROWS:]
        hv_ref[...] = v_raw[rows - HIST_ROWS:]
        for h in heads:
            state_ref[h] = states[h]
        blocks = []
        for ci in range(rows // c):
            gated = []
            for h in heads:
                o = outs[ci * N_HEADS + h]
                o = o * lax.rsqrt(jnp.mean(o * o, axis=-1, keepdims=True) + RMS_EPS)
                sl = slice(h * hd, (h + 1) * hd)
                z = zs_ref[ci * c:(ci + 1) * c, sl].astype(_f32)
                gated.append((o * onw_ref[...] * z).astype(obuf_ref.dtype))
            blocks.append(jnp.concatenate(gated, axis=1))
        obuf_ref[slot] = jnp.concatenate(blocks, axis=0)

    @pl.when(s == 0)
    def _():
        mq, mk, mv = mq_ref[...], mk_ref[...], mv_ref[...]
        zeros = jnp.zeros((HIST_ROWS, N_HEADS * hd), _f32)
        is_meta = lax.broadcasted_iota(jnp.int32, (c, 128), 0) >= c - N_META
        _, states = _run_interleaved(delta_step(
            mq, mk, mv, mba_ref[...], zeros, zeros, zeros, is_meta,
            [jnp.zeros((hd, hd), _f32) for _ in heads]))
        for h in heads:
            state_ref[h] = states[h]
        hq_ref[...] = mq[c - HIST_ROWS:]
        hk_ref[...] = mk[c - HIST_ROWS:]
        hv_ref[...] = mv[c - HIST_ROWS:]
        step(True, False)

    @pl.when(jnp.logical_and(s > 0, s < nblk))
    def _():
        step(True, True)

    @pl.when(s == nblk)
    def _():
        step(False, True)


def _mixer(lin, act, ba, meta_lin, meta_lin_pad, meta_ba_pad, conv_w, alog_row, dtb_row, onw_row, x,
           pool_w, pool_scale, w_proj_a, w_proj_b, w_out, ffn_norm_w, batch, p_part, zs_part):
    m, d = x.shape
    c = DELTA_CHUNK
    rows = MIXER_ROWS
    assert (m // batch) % rows == 0 and rows % c == 0
    nblk = m // batch // rows
    assert nblk >= 2
    hd = HEAD_DIM
    w = N_HEADS * hd
    pd = pool_w.shape[0] * POOL_GROUP_DIM
    assert pd == w
    sub = rows // N_META

    def delta_row(b, s):
        return b * nblk + jnp.minimum(s, nblk - 1)

    def merge_row(b, s):
        return b * nblk + jnp.maximum(s - 1, 0)

    def tok(part):
        return pl.BlockSpec((rows, w), lambda b, s: (delta_row(b, s), part))

    def meta(part):
        return pl.BlockSpec((c, w), lambda b, s: (0, part))

    def cw(part):
        return pl.BlockSpec((CONV_K, w), lambda b, s: (0, part))

    def const(shape):
        return pl.BlockSpec(shape, lambda b, s: (0,) * len(shape), pipeline_mode=pl.Buffered(1))

    row128 = pl.BlockSpec((1, 128), lambda b, s: (0, 0))
    return pl.pallas_call(
        functools.partial(_mixer_kernel, nblk),
        grid=(batch, nblk + 1),
        in_specs=[
            tok(0), tok(1), tok(2), tok(zs_part),
            pl.BlockSpec((rows, 128), lambda b, s: (delta_row(b, s), 0)),
            meta(0), meta(1), meta(2),
            pl.BlockSpec((c, 128), lambda b, s: (0, 0)),
            cw(0), cw(1), cw(2),
            row128, row128, row128,
            pl.BlockSpec((rows, pd), lambda b, s: (merge_row(b, s), p_part)),
            pl.BlockSpec((N_META, pd), lambda b, s: (jnp.maximum(merge_row(b, s) * sub - 1, 0), p_part)),
            pl.BlockSpec((N_META, pd), lambda b, s: (0, p_part)),
            pl.BlockSpec((rows, d), lambda b, s: (merge_row(b, s), 0)),
            pl.BlockSpec((rows, d), lambda b, s: (merge_row(b, s), 1)),
            pl.BlockSpec((rows, d), lambda b, s: (merge_row(b, s), 0)),
            const(pool_w.shape), const((1, pd)), const(w_proj_a.shape), const(w_proj_b.shape),
            const(w_out.shape), const((1, d)),
        ],
        out_specs=[pl.BlockSpec((rows, d), lambda b, s: (merge_row(b, s), 0)),
                   pl.BlockSpec((rows, d), lambda b, s: (merge_row(b, s), 0))],
        out_shape=[jax.ShapeDtypeStruct((m, d), _f32), jax.ShapeDtypeStruct((m, d), _bf16)],
        scratch_shapes=[
            pltpu.VMEM((N_HEADS, hd, hd), _f32),
            pltpu.VMEM((HIST_ROWS, w), _f32),
            pltpu.VMEM((HIST_ROWS, w), _f32),
            pltpu.VMEM((HIST_ROWS, w), _f32),
            pltpu.VMEM((2, rows, w), _bf16),
        ],
        compiler_params=pltpu.CompilerParams(
            dimension_semantics=("arbitrary", "arbitrary"),
            vmem_limit_bytes=_V7X_VMEM_LIMIT),
        name="mixer",
    )(lin, lin, lin, act, ba, meta_lin_pad, meta_lin_pad, meta_lin_pad, meta_ba_pad,
      conv_w, conv_w, conv_w, alog_row, dtb_row, onw_row,
      lin, lin, meta_lin, act, act, x,
      pool_w, pool_scale, w_proj_a, w_proj_b, w_out, ffn_norm_w)


def _ffn_kernel(h_hbm, hn_ref, wg_ref, wu_ref, wd_ref, fw_ref, out_ref, h_buf, h_sem):
    i = pl.program_id(0)
    f = pl.program_id(1)
    tm = out_ref.shape[0]

    def residual_copy():
        rows = pl.ds(pl.multiple_of(i * tm, tm), tm)
        return pltpu.make_async_copy(h_hbm.at[rows], h_buf, h_sem)

    d = out_ref.shape[1]
    rc, dc = min(tm, 512), min(d, 512)

    def accumulate(r0, rows_per_chunk, first):
        rows = slice(r0, r0 + rows_per_chunk)
        hn = hn_ref[rows, :]
        act = (_silu(_dot(hn, wg_ref[...])) * _dot(hn, wu_ref[...])).astype(_bf16)
        for c0 in range(0, d, dc):
            part = _dot(act, wd_ref[:, c0:c0 + dc])
            if first:
                out_ref[rows, c0:c0 + dc] = part
            else:
                out_ref[rows, c0:c0 + dc] += part

    last = pl.num_programs(1) - 1

    @pl.when(f == 0)
    def _():
        residual_copy().start()
        for r0 in range(0, tm, rc):
            accumulate(r0, rc, True)

    @pl.when(jnp.logical_and(f > 0, f < last))
    def _():
        for r0 in range(0, tm, rc):
            accumulate(r0, rc, False)

    @pl.when(f == last)
    def _():
        residual_copy().wait()
        fc = min(tm, NORM_OVERLAP_ROWS)
        for r0 in range(0, tm, fc):
            accumulate(r0, fc, False)
            rows = slice(r0, r0 + fc)
            out_ref[rows, :] = _rms_norm(h_buf[rows, :] + out_ref[rows, :], fw_ref[...])


def _ffn(h, hn, w_gate, w_up, w_down, final_w, tm, tf):
    m, d = h.shape
    f = w_gate.shape[1]
    assert f // tf >= 2, "the first and the last ffn column step must be different steps"
    return pl.pallas_call(
        _ffn_kernel,
        grid=(m // tm, f // tf),
        in_specs=[
            pl.BlockSpec(memory_space=pl.ANY),
            pl.BlockSpec((tm, d), lambda i, j: (i, 0)),
            pl.BlockSpec((d, tf), lambda i, j: (0, j)),
            pl.BlockSpec((d, tf), lambda i, j: (0, j)),
            pl.BlockSpec((tf, d), lambda i, j: (j, 0)),
            pl.BlockSpec((1, d), lambda i, j: (0, 0)),
        ],
        out_specs=pl.BlockSpec((tm, d), lambda i, j: (i, 0)),
        out_shape=jax.ShapeDtypeStruct((m, d), _f32),
        scratch_shapes=[pltpu.VMEM((tm, d), _f32), pltpu.SemaphoreType.DMA(())],
        compiler_params=pltpu.CompilerParams(
            dimension_semantics=("arbitrary", "arbitrary"),
            vmem_limit_bytes=_V7X_VMEM_LIMIT),
        name="ffn",
    )(h, hn, w_gate, w_up, w_down, final_w)


def _largest_tile(n, cap):
    t = cap
    while n % t:
        t //= 2
    return t


def kernel(x, meta_tokens, norm_mix_w, w_in, conv_w, a_log, dt_bias, o_norm_w, w_proj_a, pool_w,
           pool_scale, w_proj_b, w_out, norm_ffn_w, w_ffn_gate, w_ffn_up, w_ffn_down, norm_final_w):
    assert w_in.shape[0] == 1, "single layer block"
    batch, seq, d = x.shape
    h, hd = N_HEADS, HEAD_DIM
    qk = h * hd
    pd = pool_w.shape[1] * POOL_GROUP_DIM
    assert seq % DELTA_CHUNK == 0 and N_META <= DELTA_CHUNK
    m = batch * seq

    w_t = w_in[0].T
    off_z = 3 * qk
    off_ba = 4 * qk
    off_p = off_ba + 2 * h
    off_g = off_p + pd
    tn = qk
    lin_cols, sig_cols = off_z + pd, 2 * d
    assert pd == qk and sig_cols % qk == 0
    row_offsets = ([0, qk, 2 * qk, off_p] + [off_g + c0 for c0 in range(0, sig_cols, tn)] + [off_z])
    w_main_t = _w_prep(w_t, row_offsets, tn, _largest_tile(d, 1024))
    w_ba_t = jnp.pad(w_t[off_ba:off_p], ((0, 128 - 2 * h), (0, 0))).astype(_bf16)
    p_part, zs_part = off_z // pd, sig_cols // qk

    x2 = x.reshape(m, d)
    nw = norm_mix_w[0].reshape(1, d)
    meta_lin, meta_ba = _in_proj(meta_tokens, nw, w_main_t, w_ba_t, N_META, tn, lin_cols, sig_cols,
                                 lin_only=True)
    lin, act, ba, wg_bf, wu_bf, wd_bf, wpa_bf, wpb_bf, wout_bf = _in_proj(
        x2, nw, w_main_t, w_ba_t, _largest_tile(m, 1024), tn, lin_cols, sig_cols,
        casts=(w_ffn_gate[0], w_ffn_up[0], w_ffn_down[0], w_proj_a[0], w_proj_b[0], w_out[0]))
    pad = ((DELTA_CHUNK - N_META, 0), (0, 0))
    meta_lin_pad = jnp.pad(meta_lin, pad)
    meta_ba_pad = jnp.pad(meta_ba, pad)

    lane_pad = lambda a: jnp.pad(a.reshape(1, h), ((0, 0), (h, 128 - 2 * h)))
    h1, hn2 = _mixer(lin, act, ba, meta_lin, meta_lin_pad, meta_ba_pad, conv_w[0], lane_pad(a_log[0]),
                     lane_pad(dt_bias[0]), o_norm_w[0].reshape(1, hd), x2, pool_w[0].astype(_bf16),
                     pool_scale[0].reshape(1, pd), wpa_bf, wpb_bf, wout_bf,
                     norm_ffn_w[0].reshape(1, d), batch, p_part, zs_part)

    out = _ffn(h1, hn2, wg_bf, wu_bf, wd_bf, norm_final_w.reshape(1, d), _largest_tile(m, 1024),
               _largest_tile(w_ffn_gate.shape[2], 512))
    return out.reshape(batch, seq, d)
```

```python
import functools

import jax
import jax.numpy as jnp
from jax import lax
from jax.experimental import pallas as pl
from jax.experimental.pallas import tpu as pltpu

N_META = 16
N_HEADS = 8
HEAD_DIM = 128
CONV_K = 4
POOL_WINDOWS = (2, 4, 8, 16)
POOL_GROUP_DIM = 256
RMS_EPS = 1e-6
L2_EPS = 1e-6

DELTA_CHUNK = 128
INV_BASE = 4
HIST_ROWS = 8
MIXER_ROWS = 256

_V7X_VMEM_LIMIT = 56 * 1024 * 1024

_f32 = jnp.float32
_bf16 = jnp.bfloat16


def _dot(a, b):
    return jnp.dot(a, b, preferred_element_type=_f32)


def _dot_nt(a, b):
    return lax.dot_general(a, b, (((1,), (1,)), ((), ())), preferred_element_type=_f32)


def _sigmoid(x):
    return 1.0 / (1.0 + jnp.exp(-x))


def _silu(x):
    return x * _sigmoid(x)


def _softplus(x):
    return jnp.maximum(x, 0.0) + jnp.log1p(jnp.exp(-jnp.abs(x)))


def _rms_norm(x, w):
    return x * lax.rsqrt(jnp.mean(x * x, axis=-1, keepdims=True) + RMS_EPS) * w


def _w_prep_kernel(w_ref, o_ref):
    o_ref[...] = w_ref[...].astype(o_ref.dtype)


def _w_prep(w_t, row_offsets, tn, tk):
    n_src, k = w_t.shape
    assert all(off % 8 == 0 and off + tn <= n_src for off in row_offsets)

    def row_off(j):
        tile = 0
        for jj, o in enumerate(row_offsets):
            tile = jnp.where(j == jj, o // 8, tile)
        return tile * 8

    return pl.pallas_call(
        _w_prep_kernel,
        grid=(len(row_offsets), k // tk),
        in_specs=[pl.BlockSpec((pl.Element(tn), pl.Element(tk)),
                               lambda j, c: (row_off(j), c * tk))],
        out_specs=pl.BlockSpec((tn, tk), lambda j, c: (j, c)),
        out_shape=jax.ShapeDtypeStruct((len(row_offsets) * tn, k), _bf16),
        compiler_params=pltpu.CompilerParams(
            dimension_semantics=("arbitrary", "arbitrary"),
            vmem_limit_bytes=_V7X_VMEM_LIMIT),
        name="w_prep",
    )(w_t)


CAST_COL_SPLIT = 4
NORM_OVERLAP_ROWS = 256


def _sigmoid_tanh(x):
    return 0.5 * jnp.tanh(0.5 * x) + 0.5


def _in_proj_kernel(n_lin, n_sig, lin_only, cast_starts, x_ref, nw_ref, wba_ref, w_ref, *refs):
    n_cast = len(cast_starts)
    cast_src, refs = refs[:n_cast], list(refs[n_cast:])
    lin_ref = refs.pop(0)
    act_ref = None if lin_only else refs.pop(0)
    ba_ref = refs.pop(0)
    cast_dst, hn_ref = refs[:n_cast], refs[n_cast]
    j = pl.program_id(1)
    tm = x_ref.shape[0]
    rc = min(tm, NORM_OVERLAP_ROWS)

    windows = [(start, start + CAST_COL_SPLIT) for start in cast_starts]
    branch_windows = [(0, n_lin), (n_lin, n_lin + n_sig)]

    def side_casts(window):
        for win, src, dst in zip(windows, cast_src, cast_dst):
            if win == window:
                dst[...] = src[...].astype(dst.dtype)

    @pl.when(j == 0)
    def _():
        for r0 in range(0, tm, rc):
            rows = slice(r0, r0 + rc)
            hn = _rms_norm(x_ref[rows, :], nw_ref[...]).astype(_bf16)
            hn_ref[rows, :] = hn
            ba_ref[rows, :] = _dot_nt(hn, wba_ref[...])
            lin_ref[rows, :] = _dot_nt(hn, w_ref[...])
        side_casts(branch_windows[0])

    @pl.when(jnp.logical_and(j > 0, j < n_lin))
    def _():
        lin_ref[...] = _dot_nt(hn_ref[...], w_ref[...])
        side_casts(branch_windows[0])

    if not lin_only:
        @pl.when(jnp.logical_and(j >= n_lin, j < n_lin + n_sig))
        def _():
            act_ref[...] = _sigmoid_tanh(_dot_nt(hn_ref[...], w_ref[...])).astype(act_ref.dtype)
            side_casts(branch_windows[1])

        @pl.when(j >= n_lin + n_sig)
        def _():
            y = _dot_nt(hn_ref[...], w_ref[...])
            act_ref[...] = (y * _sigmoid_tanh(y)).astype(act_ref.dtype)

    for win, src, dst in zip(windows, cast_src, cast_dst):
        if win not in branch_windows:
            @pl.when(jnp.logical_and(j >= win[0], j < win[1]))
            def _(src=src, dst=dst):
                dst[...] = src[...].astype(dst.dtype)


def _in_proj(x, norm_w, w_main_t, w_ba_t, tm, tn, lin_cols, sig_cols, casts=(), lin_only=False):
    m, d = x.shape
    n = lin_cols if lin_only else w_main_t.shape[0]
    assert lin_cols % tn == 0 and sig_cols % tn == 0 and n % tn == 0
    n_lin, n_sig = lin_cols // tn, sig_cols // tn
    gi, gj = m // tm, n // tn
    assert gj >= CAST_COL_SPLIT or not casts
    if n_lin == CAST_COL_SPLIT and n_sig == CAST_COL_SPLIT:
        load = {0: 0, n_lin: 0}
        cast_starts = []
        for a in casts:
            start = min(load, key=load.get)
            cast_starts.append(start)
            load[start] += a.size
        cast_starts = tuple(cast_starts)
    else:
        cast_starts = tuple(min(CAST_COL_SPLIT * k, gj - CAST_COL_SPLIT) for k in range(len(casts)))
    act_spec, act_shape = [], []
    if not lin_only:
        act_spec = [pl.BlockSpec((tm, tn), lambda i, j: (i, jnp.maximum(j - n_lin, 0)))]
        act_shape = [jax.ShapeDtypeStruct((m, n - lin_cols), _bf16)]
    cast_specs = []
    for start, a in zip(cast_starts, casts):
        rows, cols = a.shape
        assert rows % (16 * gi) == 0 and cols % (128 * CAST_COL_SPLIT) == 0
        cast_specs.append(pl.BlockSpec(
            (rows // gi, cols // CAST_COL_SPLIT),
            lambda i, j, start=start: (i, jnp.clip(j - start, 0, CAST_COL_SPLIT - 1))))
    return pl.pallas_call(
        functools.partial(_in_proj_kernel, n_lin, n_sig, lin_only, cast_starts),
        grid=(gi, gj),
        in_specs=[
            pl.BlockSpec((tm, d), lambda i, j: (i, 0)),
            pl.BlockSpec((1, d), lambda i, j: (0, 0)),
            pl.BlockSpec((128, d), lambda i, j: (0, 0)),
            pl.BlockSpec((tn, d), lambda i, j: (j, 0)),
        ] + cast_specs,
        out_specs=[
            pl.BlockSpec((tm, tn), lambda i, j: (i, jnp.minimum(j, n_lin - 1))),
        ] + act_spec + [
            pl.BlockSpec((tm, 128), lambda i, j: (i, 0)),
        ] + cast_specs,
        out_shape=[jax.ShapeDtypeStruct((m, lin_cols), _f32)] + act_shape + [
            jax.ShapeDtypeStruct((m, 128), _f32),
        ] + [jax.ShapeDtypeStruct(a.shape, _bf16) for a in casts],
        scratch_shapes=[pltpu.VMEM((tm, d), _bf16)],
        compiler_params=pltpu.CompilerParams(
            dimension_semantics=("arbitrary", "arbitrary"),
            vmem_limit_bytes=_V7X_VMEM_LIMIT),
        name="in_proj",
    )(x, norm_w, w_ba_t, w_main_t, *casts)


def _conv_silu(x, hist, cw):
    assert CONV_K == 4
    ext = jnp.concatenate([hist, x], axis=0)
    ext1 = pltpu.roll(ext, 1, axis=0)
    old = ext * cw[1:2] + ext1 * cw[0:1]
    acc = ext * cw[3:4] + ext1 * cw[2:3] + pltpu.roll(old, 2, axis=0)
    return _silu(acc[HIST_ROWS:])


def _l2_norm(x):
    return x * lax.rsqrt(jnp.sum(x * x, axis=-1, keepdims=True) + L2_EPS)


def _chunk_cumsum(g):
    pos = lax.broadcasted_iota(jnp.int32, g.shape, 0) % DELTA_CHUNK
    s = 1
    while s < DELTA_CHUNK:
        g = g + jnp.where(pos >= s, pltpu.roll(g, s, axis=0), 0.0)
        s *= 2
    return g


def _lane_column(x, lane_idx):
    lane = lax.broadcasted_iota(jnp.int32, x.shape, 1)
    col = jnp.sum(jnp.where(lane == lane_idx, x, 0.0), axis=1, keepdims=True)
    return jnp.broadcast_to(col, x.shape)


LEVEL_GROUP = 4
MERGE_UNIT_COLS = 256


def _level(thunks):
    out = []
    for i, thunk in enumerate(thunks):
        out.append(thunk())
        if i % LEVEL_GROUP == LEVEL_GROUP - 1 or i == len(thunks) - 1:
            yield
    return out


def _unit_lower_inverse(mats):
    c = mats[0].shape[0]
    row = lax.broadcasted_iota(jnp.int32, (c, c), 0)
    col = lax.broadcasted_iota(jnp.int32, (c, c), 1)
    apart = row ^ col
    eye = jnp.where(row == col, 1.0, 0.0)
    diag = [jnp.where(apart < INV_BASE, a, 0.0) for a in mats]
    ts = [eye - a for a in diag]
    apows = yield from _level([lambda a=a: _dot(a.astype(_bf16), a.astype(_bf16)) for a in diag])
    n = 2
    while n < INV_BASE:
        apbs = [ap.astype(_bf16) for ap in apows]
        if 2 * n < INV_BASE:
            both = yield from _level(
                [lambda t=t, apb=apb: _dot(jnp.concatenate([t.astype(_bf16), apb], axis=0), apb)
                 for t, apb in zip(ts, apbs)])
            ts = [t + b[:c] for t, b in zip(ts, both)]
            apows = [b[c:] for b in both]
        else:
            ts = yield from _level([lambda t=t, apb=apb: t + _dot(t.astype(_bf16), apb)
                                    for t, apb in zip(ts, apbs)])
        n *= 2
    size = INV_BASE
    while size < c:
        lower_left = jnp.logical_and(apart >= size, apart < 2 * size)
        tbs = [t.astype(_bf16) for t in ts]
        xs = yield from _level([lambda a=a, tb=tb: _dot(jnp.where(lower_left, a, 0.0).astype(_bf16), tb)
                                for a, tb in zip(mats, tbs)])
        ts = yield from _level([lambda t=t, tb=tb, x=x: t - _dot(tb, x.astype(_bf16))
                                for t, tb, x in zip(ts, tbs, xs)])
        size *= 2
    return ts


def _delta_chunks(probs, states):
    c = DELTA_CHUNK
    row = lax.broadcasted_iota(jnp.int32, (c, c), 0)
    col = lax.broadcasted_iota(jnp.int32, (c, c), 1)
    qn, kn, v, beta, gc = (list(t) for t in zip(*probs))
    n = len(probs)
    n_heads = len(states)
    rng = range(n)
    decay = [jnp.exp(jnp.where(row >= col, gc[i] - gc[i].T, -jnp.inf)) for i in rng]
    qs = [qn[i] * (HEAD_DIM ** -0.5) for i in rng]
    kb = [kn[i].astype(_bf16) for i in rng]
    qk_kk = yield from _level(
        [lambda i=i: _dot_nt(jnp.concatenate([qs[i].astype(_bf16), kb[i]], axis=0), kb[i]) for i in rng])
    qk = [(qk_kk[i][:c] * decay[i]).astype(_bf16) for i in rng]
    a_mat = [jnp.where(row > col, qk_kk[i][c:] * beta[i] * decay[i], 0.0) for i in rng]
    t_mat = yield from _unit_lower_inverse(a_mat)
    eg = [jnp.exp(gc[i]) for i in rng]
    rhs = [jnp.concatenate([v[i] * beta[i], kn[i] * (beta[i] * eg[i])], axis=1).astype(_bf16)
           for i in rng]
    uw = yield from _level([lambda i=i: _dot(t_mat[i].astype(_bf16), rhs[i]) for i in rng])
    g_last = [gc[i][c - 1:c] for i in rng]
    k_dec_t = [(kn[i] * jnp.exp(g_last[i] - gc[i])).T.astype(_bf16) for i in rng]
    wq = [jnp.concatenate([uw[i][:, HEAD_DIM:], qs[i] * eg[i]], axis=0).astype(_bf16) for i in rng]
    outs = []
    for first in range(0, n, n_heads):
        heads = range(n_heads)
        ws_qs = yield from _level(
            [lambda h=h: _dot(wq[first + h], states[h].astype(_bf16)) for h in heads])
        vb = [(uw[first + h][:, :HEAD_DIM] - ws_qs[h][:c]).astype(_bf16) for h in heads]
        outs += yield from _level(
            [lambda h=h: ws_qs[h][c:] + _dot(qk[first + h], vb[h]) for h in heads])
        states = yield from _level(
            [lambda h=h: states[h] * jnp.exp(g_last[first + h]) + _dot(k_dec_t[first + h], vb[h])
             for h in heads])
    return outs, states


def _spread(n_items, n_slots):
    return [(i + 1) * n_items // n_slots - i * n_items // n_slots for i in range(n_slots)]


def _run_interleaved(gen, fillers=(), plan=()):
    fillers = list(fillers)
    for count in list(plan) + [0] * 256:
        try:
            next(gen)
        except StopIteration as stop:
            for f in fillers:
                f()
            return stop.value
        for _ in range(min(count, len(fillers))):
            fillers.pop(0)()
    raise AssertionError("generator yielded more often than planned for")


def _merge_pieces(first, p_ref, pprev_ref, pmeta_ref, sa_ref, sb_ref, o_gated, x_ref,
                  poolw_ref, pscale_ref, wpa_ref, wpb_ref, wout_ref, nw_ref, out_ref, hn_ref):
    d = x_ref.shape[1]
    unit = MERGE_UNIT_COLS
    cols = range(0, d, unit)
    val = {}

    def pool():
        prev = jnp.where(first, pmeta_ref[...], pprev_ref[...])
        p = p_ref[...]
        ext = jnp.concatenate([prev, p], axis=0)
        gd = POOL_GROUP_DIM
        zb = []
        for gi, win in enumerate(POOL_WINDOWS):
            acc = ext[:, gi * gd:(gi + 1) * gd]
            s = 1
            while s < win:
                acc = acc + pltpu.roll(acc, s, axis=0)
                s *= 2
            pooled = acc[N_META:] * (1.0 / win) - p[:, gi * gd:(gi + 1) * gd]
            zb.append(_dot(pooled.astype(_bf16), poolw_ref[gi]))
        val["zb"] = (jnp.concatenate(zb, axis=1) * pscale_ref[...]).astype(_bf16)

    def branch(name, lhs, w_ref, gate_ref, c0):
        def run():
            y = _dot(lhs(), w_ref[:, c0:c0 + unit])
            val[name, c0] = gate_ref[:, c0:c0 + unit].astype(_f32) * y
        return run

    def out_cols(c0):
        def run():
            if "merged" not in val:
                merged = [val["a", m0] + val["b", m0] for m0 in cols]
                val["merged"] = jnp.concatenate(merged, axis=1).astype(_bf16)
            val["h", c0] = x_ref[:, c0:c0 + unit] + _dot(val["merged"], wout_ref[:, c0:c0 + unit])
        return run

    def finish():
        h1 = jnp.concatenate([val["h", c0] for c0 in cols], axis=1)
        out_ref[...] = h1
        hn_ref[...] = _rms_norm(h1, nw_ref[...]).astype(hn_ref.dtype)

    branches = ([branch("b", lambda: val["zb"], wpb_ref, sb_ref, c0) for c0 in cols]
                + [branch("a", o_gated, wpa_ref, sa_ref, c0) for c0 in cols])
    return [pool] + branches, [out_cols(c0) for c0 in cols] + [finish]


def _mixer_kernel(nblk, q_ref, k_ref, v_ref, zs_ref, ba_ref, mq_ref, mk_ref, mv_ref, mba_ref,
                  cwq_ref, cwk_ref, cwv_ref, alog_ref, dtb_ref, onw_ref,
                  p_ref, pprev_ref, pmeta_ref, sa_ref, sb_ref, x_ref,
                  poolw_ref, pscale_ref, wpa_ref, wpb_ref, wout_ref, nw_ref,
                  out_ref, hn_ref, state_ref, hq_ref, hk_ref, hv_ref, obuf_ref):
    c = DELTA_CHUNK
    hd = HEAD_DIM
    heads = range(N_HEADS)
    rows = q_ref.shape[0]
    s = pl.program_id(1)

    def delta_step(q_raw, k_raw, v_raw, ba, hq, hk, hv, valid, states):
        qc = _conv_silu(q_raw, hq, cwq_ref[...])
        yield
        kc = _conv_silu(k_raw, hk, cwk_ref[...])
        yield
        vc = _conv_silu(v_raw, hv, cwv_ref[...])
        yield
        beta_all = _sigmoid(ba)
        g_all = -jnp.exp(alog_ref[...]) * _softplus(ba + dtb_ref[...])
        if valid is not None:
            beta_all = jnp.where(valid, beta_all, 0.0)
            g_all = jnp.where(valid, g_all, 0.0)
        gc_all = _chunk_cumsum(g_all)
        probs = []
        for r0 in range(0, ba.shape[0], c):
            rs = slice(r0, r0 + c)
            for h in heads:
                sl = slice(h * hd, (h + 1) * hd)
                probs.append((_l2_norm(qc[rs, sl]), _l2_norm(kc[rs, sl]), vc[rs, sl],
                              _lane_column(beta_all[rs], h), _lane_column(gc_all[rs], N_HEADS + h)))
        yield
        return (yield from _delta_chunks(probs, states))

    @pl.when(s == 0)
    def _():
        mq, mk, mv = mq_ref[...], mk_ref[...], mv_ref[...]
        zeros = jnp.zeros((HIST_ROWS, N_HEADS * hd), _f32)
        is_meta = lax.broadcasted_iota(jnp.int32, (c, 128), 0) >= c - N_META
        _, states = _run_interleaved(delta_step(
            mq, mk, mv, mba_ref[...], zeros, zeros, zeros, is_meta,
            [jnp.zeros((hd, hd), _f32) for _ in heads]))
        for h in heads:
            state_ref[h] = states[h]
        hq_ref[...] = mq[c - HIST_ROWS:]
        hk_ref[...] = mk[c - HIST_ROWS:]
        hv_ref[...] = mv[c - HIST_ROWS:]
        obuf_ref[...] = jnp.zeros_like(obuf_ref)

    q_raw, k_raw, v_raw = q_ref[...], k_ref[...], v_ref[...]
    slot = s % 2
    early, late = _merge_pieces(s <= 1, p_ref, pprev_ref, pmeta_ref, sa_ref, sb_ref,
                                lambda: obuf_ref[1 - slot],
                                x_ref, poolw_ref, pscale_ref, wpa_ref, wpb_ref, wout_ref, nw_ref,
                                out_ref, hn_ref)
    base, extra = divmod(len(early), 4)
    n_level_yields = 15 * (rows // c) * N_HEADS // LEVEL_GROUP
    outs, states = _run_interleaved(
        delta_step(q_raw, k_raw, v_raw, ba_ref[...], hq_ref[...], hk_ref[...], hv_ref[...], None,
                   [state_ref[h] for h in heads]),
        early + late,
        plan=[base + 1] * extra + [base] * (4 - extra) + _spread(len(late), n_level_yields))
    hq_ref[...] = q_raw[rows - HIST_ROWS:]
    hk_ref[...] = k_raw[rows - HIST_ROWS:]
    hv_ref[...] = v_raw[rows - HIST_ROWS:]
    for h in heads:
        state_ref[h] = states[h]
    blocks = []
    for ci in range(rows // c):
        gated = []
        for h in heads:
            o = outs[ci * N_HEADS + h]
            o = o * lax.rsqrt(jnp.mean(o * o, axis=-1, keepdims=True) + RMS_EPS)
            sl = slice(h * hd, (h + 1) * hd)
            z = zs_ref[ci * c:(ci + 1) * c, sl].astype(_f32)
            gated.append((o * onw_ref[...] * z).astype(obuf_ref.dtype))
        blocks.append(jnp.concatenate(gated, axis=1))
    obuf_ref[slot] = jnp.concatenate(blocks, axis=0)


def _mixer(lin, act, ba, meta_lin, meta_lin_pad, meta_ba_pad, conv_w, alog_row, dtb_row, onw_row, x,
           pool_w, pool_scale, w_proj_a, w_proj_b, w_out, ffn_norm_w, batch, p_part, zs_part):
    m, d = x.shape
    c = DELTA_CHUNK
    rows = MIXER_ROWS
    assert (m // batch) % rows == 0 and rows % c == 0
    nblk = m // batch // rows
    hd = HEAD_DIM
    w = N_HEADS * hd
    pd = pool_w.shape[0] * POOL_GROUP_DIM
    assert pd == w
    sub = rows // N_META

    def delta_row(b, s):
        return b * nblk + jnp.minimum(s, nblk - 1)

    def merge_row(b, s):
        return b * nblk + jnp.maximum(s - 1, 0)

    def tok(part):
        return pl.BlockSpec((rows, w), lambda b, s: (delta_row(b, s), part))

    def meta(part):
        return pl.BlockSpec((c, w), lambda b, s: (0, part))

    def cw(part):
        return pl.BlockSpec((CONV_K, w), lambda b, s: (0, part))

    def const(shape):
        return pl.BlockSpec(shape, lambda b, s: (0,) * len(shape), pipeline_mode=pl.Buffered(1))

    row128 = pl.BlockSpec((1, 128), lambda b, s: (0, 0))
    return pl.pallas_call(
        functools.partial(_mixer_kernel, nblk),
        grid=(batch, nblk + 1),
        in_specs=[
            tok(0), tok(1), tok(2), tok(zs_part),
            pl.BlockSpec((rows, 128), lambda b, s: (delta_row(b, s), 0)),
            meta(0), meta(1), meta(2),
            pl.BlockSpec((c, 128), lambda b, s: (0, 0)),
            cw(0), cw(1), cw(2),
            row128, row128, row128,
            pl.BlockSpec((rows, pd), lambda b, s: (merge_row(b, s), p_part)),
            pl.BlockSpec((N_META, pd), lambda b, s: (jnp.maximum(merge_row(b, s) * sub - 1, 0), p_part)),
            pl.BlockSpec((N_META, pd), lambda b, s: (0, p_part)),
            pl.BlockSpec((rows, d), lambda b, s: (merge_row(b, s), 0)),
            pl.BlockSpec((rows, d), lambda b, s: (merge_row(b, s), 1)),
            pl.BlockSpec((rows, d), lambda b, s: (merge_row(b, s), 0)),
            const(pool_w.shape), const((1, pd)), const(w_proj_a.shape), const(w_proj_b.shape),
            const(w_out.shape), const((1, d)),
        ],
        out_specs=[pl.BlockSpec((rows, d), lambda b, s: (merge_row(b, s), 0)),
                   pl.BlockSpec((rows, d), lambda b, s: (merge_row(b, s), 0))],
        out_shape=[jax.ShapeDtypeStruct((m, d), _f32), jax.ShapeDtypeStruct((m, d), _bf16)],
        scratch_shapes=[
            pltpu.VMEM((N_HEADS, hd, hd), _f32),
            pltpu.VMEM((HIST_ROWS, w), _f32),
            pltpu.VMEM((HIST_ROWS, w), _f32),
            pltpu.VMEM((HIST_ROWS, w), _f32),
            pltpu.VMEM((2, rows, w), _bf16),
        ],
        compiler_params=pltpu.CompilerParams(
            dimension_semantics=("arbitrary", "arbitrary"),
            vmem_limit_bytes=_V7X_VMEM_LIMIT),
        name="mixer",
    )(lin, lin, lin, act, ba, meta_lin_pad, meta_lin_pad, meta_lin_pad, meta_ba_pad,
      conv_w, conv_w, conv_w, alog_row, dtb_row, onw_row,
      lin, lin, meta_lin, act, act, x,
      pool_w, pool_scale, w_proj_a, w_proj_b, w_out, ffn_norm_w)


def _ffn_kernel(h_hbm, hn_ref, wg_ref, wu_ref, wd_ref, fw_ref, out_ref, h_buf, h_sem):
    i = pl.program_id(0)
    f = pl.program_id(1)
    tm = out_ref.shape[0]

    def residual_copy():
        rows = pl.ds(pl.multiple_of(i * tm, tm), tm)
        return pltpu.make_async_copy(h_hbm.at[rows], h_buf, h_sem)

    d = out_ref.shape[1]
    rc, dc = min(tm, 512), min(d, 512)

    def accumulate(r0, rows_per_chunk, first):
        rows = slice(r0, r0 + rows_per_chunk)
        hn = hn_ref[rows, :]
        act = (_silu(_dot(hn, wg_ref[...])) * _dot(hn, wu_ref[...])).astype(_bf16)
        for c0 in range(0, d, dc):
            part = _dot(act, wd_ref[:, c0:c0 + dc])
            if first:
                out_ref[rows, c0:c0 + dc] = part
            else:
                out_ref[rows, c0:c0 + dc] += part

    last = pl.num_programs(1) - 1

    @pl.when(f == 0)
    def _():
        residual_copy().start()
        for r0 in range(0, tm, rc):
            accumulate(r0, rc, True)

    @pl.when(jnp.logical_and(f > 0, f < last))
    def _():
        for r0 in range(0, tm, rc):
            accumulate(r0, rc, False)

    @pl.when(f == last)
    def _():
        residual_copy().wait()
        fc = min(tm, NORM_OVERLAP_ROWS)
        for r0 in range(0, tm, fc):
            accumulate(r0, fc, False)
            rows = slice(r0, r0 + fc)
            out_ref[rows, :] = _rms_norm(h_buf[rows, :] + out_ref[rows, :], fw_ref[...])


def _ffn(h, hn, w_gate, w_up, w_down, final_w, tm, tf):
    m, d = h.shape
    f = w_gate.shape[1]
    assert f // tf >= 2, "the first and the last ffn column step must be different steps"
    return pl.pallas_call(
        _ffn_kernel,
        grid=(m // tm, f // tf),
        in_specs=[
            pl.BlockSpec(memory_space=pl.ANY),
            pl.BlockSpec((tm, d), lambda i, j: (i, 0)),
            pl.BlockSpec((d, tf), lambda i, j: (0, j)),
            pl.BlockSpec((d, tf), lambda i, j: (0, j)),
            pl.BlockSpec((tf, d), lambda i, j: (j, 0)),
            pl.BlockSpec((1, d), lambda i, j: (0, 0)),
        ],
        out_specs=pl.BlockSpec((tm, d), lambda i, j: (i, 0)),
        out_shape=jax.ShapeDtypeStruct((m, d), _f32),
        scratch_shapes=[pltpu.VMEM((tm, d), _f32), pltpu.SemaphoreType.DMA(())],
        compiler_params=pltpu.CompilerParams(
            dimension_semantics=("arbitrary", "arbitrary"),
            vmem_limit_bytes=_V7X_VMEM_LIMIT),
        name="ffn",
    )(h, hn, w_gate, w_up, w_down, final_w)


def _largest_tile(n, cap):
    t = cap
    while n % t:
        t //= 2
    return t


def kernel(x, meta_tokens, norm_mix_w, w_in, conv_w, a_log, dt_bias, o_norm_w, w_proj_a, pool_w,
           pool_scale, w_proj_b, w_out, norm_ffn_w, w_ffn_gate, w_ffn_up, w_ffn_down, norm_final_w):
    assert w_in.shape[0] == 1, "single layer block"
    batch, seq, d = x.shape
    h, hd = N_HEADS, HEAD_DIM
    qk = h * hd
    pd = pool_w.shape[1] * POOL_GROUP_DIM
    assert seq % DELTA_CHUNK == 0 and N_META <= DELTA_CHUNK
    m = batch * seq

    w_t = w_in[0].T
    off_z = 3 * qk
    off_ba = 4 * qk
    off_p = off_ba + 2 * h
    off_g = off_p + pd
    tn = qk
    lin_cols, sig_cols = off_z + pd, 2 * d
    assert pd == qk and sig_cols % qk == 0
    row_offsets = ([0, qk, 2 * qk, off_p] + [off_g + c0 for c0 in range(0, sig_cols, tn)] + [off_z])
    w_main_t = _w_prep(w_t, row_offsets, tn, _largest_tile(d, 1024))
    w_ba_t = jnp.pad(w_t[off_ba:off_p], ((0, 128 - 2 * h), (0, 0))).astype(_bf16)
    p_part, zs_part = off_z // pd, sig_cols // qk

    x2 = x.reshape(m, d)
    nw = norm_mix_w[0].reshape(1, d)
    meta_lin, meta_ba = _in_proj(meta_tokens, nw, w_main_t, w_ba_t, N_META, tn, lin_cols, sig_cols,
                                 lin_only=True)
    lin, act, ba, wg_bf, wu_bf, wd_bf, wpa_bf, wpb_bf, wout_bf = _in_proj(
        x2, nw, w_main_t, w_ba_t, _largest_tile(m, 1024), tn, lin_cols, sig_cols,
        casts=(w_ffn_gate[0], w_ffn_up[0], w_ffn_down[0], w_proj_a[0], w_proj_b[0], w_out[0]))
    pad = ((DELTA_CHUNK - N_META, 0), (0, 0))
    meta_lin_pad = jnp.pad(meta_lin, pad)
    meta_ba_pad = jnp.pad(meta_ba, pad)

    lane_pad = lambda a: jnp.pad(a.reshape(1, h), ((0, 0), (h, 128 - 2 * h)))
    h1, hn2 = _mixer(lin, act, ba, meta_lin, meta_lin_pad, meta_ba_pad, conv_w[0], lane_pad(a_log[0]),
                     lane_pad(dt_bias[0]), o_norm_w[0].reshape(1, hd), x2, pool_w[0].astype(_bf16),
                     pool_scale[0].reshape(1, pd), wpa_bf, wpb_bf, wout_bf,
                     norm_ffn_w[0].reshape(1, d), batch, p_part, zs_part)

    out = _ffn(h1, hn2, wg_bf, wu_bf, wd_bf, norm_final_w.reshape(1, d), _largest_tile(m, 1024),
               _largest_tile(w_ffn_gate.shape[2], 512))
    return out.reshape(batch, seq, d)
```

```python
import functools

import jax
import jax.numpy as jnp
from jax import lax
from jax.experimental import pallas as pl
from jax.experimental.pallas import tpu as pltpu

N_META = 16
N_HEADS = 8
HEAD_DIM = 128
CONV_K = 4
POOL_WINDOWS = (2, 4, 8, 16)
POOL_GROUP_DIM = 256
RMS_EPS = 1e-6
L2_EPS = 1e-6

DELTA_CHUNK = 128
INV_BASE = 4
HIST_ROWS = 8
MIXER_ROWS = 256

_V7X_VMEM_LIMIT = 56 * 1024 * 1024

_f32 = jnp.float32
_bf16 = jnp.bfloat16


def _dot(a, b):
    return jnp.dot(a, b, preferred_element_type=_f32)


def _dot_nt(a, b):
    return lax.dot_general(a, b, (((1,), (1,)), ((), ())), preferred_element_type=_f32)


def _sigmoid(x):
    return 1.0 / (1.0 + jnp.exp(-x))


def _silu(x):
    return x * _sigmoid(x)


def _softplus(x):
    return jnp.maximum(x, 0.0) + jnp.log1p(jnp.exp(-jnp.abs(x)))


def _rms_norm(x, w):
    return x * lax.rsqrt(jnp.mean(x * x, axis=-1, keepdims=True) + RMS_EPS) * w


def _w_prep_kernel(w_ref, o_ref):
    o_ref[...] = w_ref[...].astype(o_ref.dtype)


def _w_prep(w_t, row_offsets, tn, tk):
    n_src, k = w_t.shape
    assert all(off % 8 == 0 and off + tn <= n_src for off in row_offsets)

    def row_off(j):
        tile = 0
        for jj, o in enumerate(row_offsets):
            tile = jnp.where(j == jj, o // 8, tile)
        return tile * 8

    return pl.pallas_call(
        _w_prep_kernel,
        grid=(len(row_offsets), k // tk),
        in_specs=[pl.BlockSpec((pl.Element(tn), pl.Element(tk)),
                               lambda j, c: (row_off(j), c * tk))],
        out_specs=pl.BlockSpec((tn, tk), lambda j, c: (j, c)),
        out_shape=jax.ShapeDtypeStruct((len(row_offsets) * tn, k), _bf16),
        compiler_params=pltpu.CompilerParams(
            dimension_semantics=("arbitrary", "arbitrary"),
            vmem_limit_bytes=_V7X_VMEM_LIMIT),
        name="w_prep",
    )(w_t)


CAST_COL_SPLIT = 4
NORM_OVERLAP_ROWS = 256


def _sigmoid_tanh(x):
    return 0.5 * jnp.tanh(0.5 * x) + 0.5


W_RING = 3


class _RingTile:
    def __init__(self, buf, slot):
        self._buf, self._slot = buf, slot

    def __getitem__(self, idx):
        assert idx is Ellipsis
        return self._buf[self._slot]


def _in_proj_kernel(n_lin, n_sig, lin_only, cast_starts, x_ref, nw_ref, wba_ref, w_hbm, *refs):
    n_cast = len(cast_starts)
    cast_src, refs = refs[:n_cast], list(refs[n_cast:])
    lin_ref = refs.pop(0)
    act_ref = None if lin_only else refs.pop(0)
    ba_ref = refs.pop(0)
    cast_dst, hn_ref, w_buf, w_sem = refs[:n_cast], refs[n_cast], refs[n_cast + 1], refs[n_cast + 2]
    j = pl.program_id(1)
    tm = x_ref.shape[0]
    rc = min(tm, NORM_OVERLAP_ROWS)

    gj = pl.num_programs(1)
    tn = w_buf.shape[1]
    step = pl.program_id(0) * gj + j
    n_steps = pl.num_programs(0) * gj

    def w_copy(t):
        rows = pl.ds(pl.multiple_of((t % gj) * tn, tn), tn)
        slot = t % W_RING
        return pltpu.make_async_copy(w_hbm.at[rows], w_buf.at[slot], w_sem.at[slot])

    @pl.when(step == 0)
    def _():
        for t in range(W_RING - 1):
            w_copy(t).start()

    @pl.when(step + (W_RING - 1) < n_steps)
    def _():
        w_copy(step + (W_RING - 1)).start()

    w_copy(step).wait()
    w_ref = _RingTile(w_buf, step % W_RING)

    windows = [(start, start + CAST_COL_SPLIT) for start in cast_starts]
    branch_windows = [(0, n_lin), (n_lin, n_lin + n_sig)]

    def side_casts(window):
        for win, src, dst in zip(windows, cast_src, cast_dst):
            if win == window:
                dst[...] = src[...].astype(dst.dtype)

    @pl.when(j == 0)
    def _():
        for r0 in range(0, tm, rc):
            rows = slice(r0, r0 + rc)
            hn = _rms_norm(x_ref[rows, :], nw_ref[...]).astype(_bf16)
            hn_ref[rows, :] = hn
            ba_ref[rows, :] = _dot_nt(hn, wba_ref[...])
            lin_ref[rows, :] = _dot_nt(hn, w_ref[...])
        side_casts(branch_windows[0])

    @pl.when(jnp.logical_and(j > 0, j < n_lin))
    def _():
        lin_ref[...] = _dot_nt(hn_ref[...], w_ref[...])
        side_casts(branch_windows[0])

    if not lin_only:
        @pl.when(jnp.logical_and(j >= n_lin, j < n_lin + n_sig))
        def _():
            act_ref[...] = _sigmoid_tanh(_dot_nt(hn_ref[...], w_ref[...])).astype(act_ref.dtype)
            side_casts(branch_windows[1])

        @pl.when(j >= n_lin + n_sig)
        def _():
            y = _dot_nt(hn_ref[...], w_ref[...])
            act_ref[...] = (y * _sigmoid_tanh(y)).astype(act_ref.dtype)

    for win, src, dst in zip(windows, cast_src, cast_dst):
        if win not in branch_windows:
            @pl.when(jnp.logical_and(j >= win[0], j < win[1]))
            def _(src=src, dst=dst):
                dst[...] = src[...].astype(dst.dtype)


def _in_proj(x, norm_w, w_main_t, w_ba_t, tm, tn, lin_cols, sig_cols, casts=(), lin_only=False):
    m, d = x.shape
    n = lin_cols if lin_only else w_main_t.shape[0]
    assert lin_cols % tn == 0 and sig_cols % tn == 0 and n % tn == 0
    n_lin, n_sig = lin_cols // tn, sig_cols // tn
    gi, gj = m // tm, n // tn
    assert gj >= CAST_COL_SPLIT or not casts
    if n_lin == CAST_COL_SPLIT and n_sig == CAST_COL_SPLIT:
        load = {0: 0, n_lin: 0}
        cast_starts = []
        for a in casts:
            start = min(load, key=load.get)
            cast_starts.append(start)
            load[start] += a.size
        cast_starts = tuple(cast_starts)
    else:
        cast_starts = tuple(min(CAST_COL_SPLIT * k, gj - CAST_COL_SPLIT) for k in range(len(casts)))
    act_spec, act_shape = [], []
    if not lin_only:
        act_spec = [pl.BlockSpec((tm, tn), lambda i, j: (i, jnp.maximum(j - n_lin, 0)))]
        act_shape = [jax.ShapeDtypeStruct((m, n - lin_cols), _bf16)]
    cast_specs = []
    for start, a in zip(cast_starts, casts):
        rows, cols = a.shape
        assert rows % (16 * gi) == 0 and cols % (128 * CAST_COL_SPLIT) == 0
        cast_specs.append(pl.BlockSpec(
            (rows // gi, cols // CAST_COL_SPLIT),
            lambda i, j, start=start: (i, jnp.clip(j - start, 0, CAST_COL_SPLIT - 1))))
    return pl.pallas_call(
        functools.partial(_in_proj_kernel, n_lin, n_sig, lin_only, cast_starts),
        grid=(gi, gj),
        in_specs=[
            pl.BlockSpec((tm, d), lambda i, j: (i, 0)),
            pl.BlockSpec((1, d), lambda i, j: (0, 0)),
            pl.BlockSpec((128, d), lambda i, j: (0, 0)),
            pl.BlockSpec(memory_space=pl.ANY),
        ] + cast_specs,
        out_specs=[
            pl.BlockSpec((tm, tn), lambda i, j: (i, jnp.minimum(j, n_lin - 1))),
        ] + act_spec + [
            pl.BlockSpec((tm, 128), lambda i, j: (i, 0)),
        ] + cast_specs,
        out_shape=[jax.ShapeDtypeStruct((m, lin_cols), _f32)] + act_shape + [
            jax.ShapeDtypeStruct((m, 128), _f32),
        ] + [jax.ShapeDtypeStruct(a.shape, _bf16) for a in casts],
        scratch_shapes=[pltpu.VMEM((tm, d), _bf16), pltpu.VMEM((W_RING, tn, d), _bf16),
                        pltpu.SemaphoreType.DMA((W_RING,))],
        compiler_params=pltpu.CompilerParams(
            dimension_semantics=("arbitrary", "arbitrary"),
            vmem_limit_bytes=_V7X_VMEM_LIMIT + 2 * 1024 * 1024),
        name="in_proj",
    )(x, norm_w, w_ba_t, w_main_t, *casts)


def _conv_silu(x, hist, cw):
    assert CONV_K == 4
    ext = jnp.concatenate([hist, x], axis=0)
    ext1 = pltpu.roll(ext, 1, axis=0)
    old = ext * cw[1:2] + ext1 * cw[0:1]
    acc = ext * cw[3:4] + ext1 * cw[2:3] + pltpu.roll(old, 2, axis=0)
    return _silu(acc[HIST_ROWS:])


def _l2_norm(x):
    return x * lax.rsqrt(jnp.sum(x * x, axis=-1, keepdims=True) + L2_EPS)


def _chunk_cumsum(g):
    pos = lax.broadcasted_iota(jnp.int32, g.shape, 0) % DELTA_CHUNK
    s = 1
    while s < DELTA_CHUNK:
        g = g + jnp.where(pos >= s, pltpu.roll(g, s, axis=0), 0.0)
        s *= 2
    return g


def _lane_column(x, lane_idx):
    lane = lax.broadcasted_iota(jnp.int32, x.shape, 1)
    col = jnp.sum(jnp.where(lane == lane_idx, x, 0.0), axis=1, keepdims=True)
    return jnp.broadcast_to(col, x.shape)


LEVEL_GROUP = 4
MERGE_UNIT_COLS = 256


def _level(thunks):
    out = []
    for i, thunk in enumerate(thunks):
        out.append(thunk())
        if i % LEVEL_GROUP == LEVEL_GROUP - 1 or i == len(thunks) - 1:
            yield
    return out


def _unit_lower_inverse(mats):
    c = mats[0].shape[0]
    row = lax.broadcasted_iota(jnp.int32, (c, c), 0)
    col = lax.broadcasted_iota(jnp.int32, (c, c), 1)
    apart = row ^ col
    eye = jnp.where(row == col, 1.0, 0.0)
    diag = [jnp.where(apart < INV_BASE, a, 0.0) for a in mats]
    ts = [eye - a for a in diag]
    apows = yield from _level([lambda a=a: _dot(a.astype(_bf16), a.astype(_bf16)) for a in diag])
    n = 2
    while n < INV_BASE:
        apbs = [ap.astype(_bf16) for ap in apows]
        if 2 * n < INV_BASE:
            both = yield from _level(
                [lambda t=t, apb=apb: _dot(jnp.concatenate([t.astype(_bf16), apb], axis=0), apb)
                 for t, apb in zip(ts, apbs)])
            ts = [t + b[:c] for t, b in zip(ts, both)]
            apows = [b[c:] for b in both]
        else:
            ts = yield from _level([lambda t=t, apb=apb: t + _dot(t.astype(_bf16), apb)
                                    for t, apb in zip(ts, apbs)])
        n *= 2
    size = INV_BASE
    while size < c:
        lower_left = jnp.logical_and(apart >= size, apart < 2 * size)
        tbs = [t.astype(_bf16) for t in ts]
        xs = yield from _level([lambda a=a, tb=tb: _dot(jnp.where(lower_left, a, 0.0).astype(_bf16), tb)
                                for a, tb in zip(mats, tbs)])
        ts = yield from _level([lambda t=t, tb=tb, x=x: t - _dot(tb, x.astype(_bf16))
                                for t, tb, x in zip(ts, tbs, xs)])
        size *= 2
    return ts


def _delta_chunks(probs, states):
    c = DELTA_CHUNK
    row = lax.broadcasted_iota(jnp.int32, (c, c), 0)
    col = lax.broadcasted_iota(jnp.int32, (c, c), 1)
    qn, kn, v, beta, gc = (list(t) for t in zip(*probs))
    n = len(probs)
    n_heads = len(states)
    rng = range(n)
    decay = [jnp.exp(jnp.where(row >= col, gc[i] - gc[i].T, -jnp.inf)) for i in rng]
    qs = [qn[i] * (HEAD_DIM ** -0.5) for i in rng]
    kb = [kn[i].astype(_bf16) for i in rng]
    qk_kk = yield from _level(
        [lambda i=i: _dot_nt(jnp.concatenate([qs[i].astype(_bf16), kb[i]], axis=0), kb[i]) for i in rng])
    qk = [(qk_kk[i][:c] * decay[i]).astype(_bf16) for i in rng]
    a_mat = [jnp.where(row > col, qk_kk[i][c:] * beta[i] * decay[i], 0.0) for i in rng]
    t_mat = yield from _unit_lower_inverse(a_mat)
    eg = [jnp.exp(gc[i]) for i in rng]
    rhs = [jnp.concatenate([v[i] * beta[i], kn[i] * (beta[i] * eg[i])], axis=1).astype(_bf16)
           for i in rng]
    uw = yield from _level([lambda i=i: _dot(t_mat[i].astype(_bf16), rhs[i]) for i in rng])
    g_last = [gc[i][c - 1:c] for i in rng]
    k_dec_t = [(kn[i] * jnp.exp(g_last[i] - gc[i])).T.astype(_bf16) for i in rng]
    wq = [jnp.concatenate([uw[i][:, HEAD_DIM:], qs[i] * eg[i]], axis=0).astype(_bf16) for i in rng]
    outs = []
    for first in range(0, n, n_heads):
        heads = range(n_heads)
        ws_qs = yield from _level(
            [lambda h=h: _dot(wq[first + h], states[h].astype(_bf16)) for h in heads])
        vb = [(uw[first + h][:, :HEAD_DIM] - ws_qs[h][:c]).astype(_bf16) for h in heads]
        outs += yield from _level(
            [lambda h=h: ws_qs[h][c:] + _dot(qk[first + h], vb[h]) for h in heads])
        states = yield from _level(
            [lambda h=h: states[h] * jnp.exp(g_last[first + h]) + _dot(k_dec_t[first + h], vb[h])
             for h in heads])
    return outs, states


def _spread(n_items, n_slots):
    return [(i + 1) * n_items // n_slots - i * n_items // n_slots for i in range(n_slots)]


def _run_interleaved(gen, fillers=(), plan=()):
    fillers = list(fillers)
    for count in list(plan) + [0] * 256:
        try:
            next(gen)
        except StopIteration as stop:
            for f in fillers:
                f()
            return stop.value
        for _ in range(min(count, len(fillers))):
            fillers.pop(0)()
    raise AssertionError("generator yielded more often than planned for")


def _merge_pieces(first, p_ref, pprev_ref, pmeta_ref, sa_ref, sb_ref, o_gated, x_ref,
                  poolw_ref, pscale_ref, wpa_ref, wpb_ref, wout_ref, nw_ref, out_ref, hn_ref):
    d = x_ref.shape[1]
    unit = MERGE_UNIT_COLS
    cols = range(0, d, unit)
    val = {}

    def pool():
        prev = jnp.where(first, pmeta_ref[...], pprev_ref[...])
        p = p_ref[...]
        ext = jnp.concatenate([prev, p], axis=0)
        gd = POOL_GROUP_DIM
        zb = []
        for gi, win in enumerate(POOL_WINDOWS):
            acc = ext[:, gi * gd:(gi + 1) * gd]
            s = 1
            while s < win:
                acc = acc + pltpu.roll(acc, s, axis=0)
                s *= 2
            pooled = acc[N_META:] * (1.0 / win) - p[:, gi * gd:(gi + 1) * gd]
            zb.append(_dot(pooled.astype(_bf16), poolw_ref[gi]))
        val["zb"] = (jnp.concatenate(zb, axis=1) * pscale_ref[...]).astype(_bf16)

    def branch(name, lhs, w_ref, gate_ref, c0):
        def run():
            y = _dot(lhs(), w_ref[:, c0:c0 + unit])
            val[name, c0] = gate_ref[:, c0:c0 + unit].astype(_f32) * y
        return run

    def out_cols(c0):
        def run():
            if "merged" not in val:
                merged = [val["a", m0] + val["b", m0] for m0 in cols]
                val["merged"] = jnp.concatenate(merged, axis=1).astype(_bf16)
            val["h", c0] = x_ref[:, c0:c0 + unit] + _dot(val["merged"], wout_ref[:, c0:c0 + unit])
        return run

    def finish():
        h1 = jnp.concatenate([val["h", c0] for c0 in cols], axis=1)
        out_ref[...] = h1
        hn_ref[...] = _rms_norm(h1, nw_ref[...]).astype(hn_ref.dtype)

    branches = ([branch("b", lambda: val["zb"], wpb_ref, sb_ref, c0) for c0 in cols]
                + [branch("a", o_gated, wpa_ref, sa_ref, c0) for c0 in cols])
    return [pool] + branches, [out_cols(c0) for c0 in cols] + [finish]


def _mixer_kernel(nblk, q_ref, k_ref, v_ref, zs_ref, ba_ref, mq_ref, mk_ref, mv_ref, mba_ref,
                  cwq_ref, cwk_ref, cwv_ref, alog_ref, dtb_ref, onw_ref,
                  p_ref, pprev_ref, pmeta_ref, sa_ref, sb_ref, x_ref,
                  poolw_ref, pscale_ref, wpa_ref, wpb_ref, wout_ref, nw_ref,
                  out_ref, hn_ref, state_ref, hq_ref, hk_ref, hv_ref, obuf_ref):
    c = DELTA_CHUNK
    hd = HEAD_DIM
    heads = range(N_HEADS)
    rows = q_ref.shape[0]
    s = pl.program_id(1)

    def delta_step(q_raw, k_raw, v_raw, ba, hq, hk, hv, valid, states):
        qc = _conv_silu(q_raw, hq, cwq_ref[...])
        yield
        kc = _conv_silu(k_raw, hk, cwk_ref[...])
        yield
        vc = _conv_silu(v_raw, hv, cwv_ref[...])
        yield
        beta_all = _sigmoid(ba)
        g_all = -jnp.exp(alog_ref[...]) * _softplus(ba + dtb_ref[...])
        if valid is not None:
            beta_all = jnp.where(valid, beta_all, 0.0)
            g_all = jnp.where(valid, g_all, 0.0)
        gc_all = _chunk_cumsum(g_all)
        probs = []
        for r0 in range(0, ba.shape[0], c):
            rs = slice(r0, r0 + c)
            for h in heads:
                sl = slice(h * hd, (h + 1) * hd)
                probs.append((_l2_norm(qc[rs, sl]), _l2_norm(kc[rs, sl]), vc[rs, sl],
                              _lane_column(beta_all[rs], h), _lane_column(gc_all[rs], N_HEADS + h)))
        yield
        return (yield from _delta_chunks(probs, states))

    @pl.when(s == 0)
    def _():
        mq, mk, mv = mq_ref[...], mk_ref[...], mv_ref[...]
        zeros = jnp.zeros((HIST_ROWS, N_HEADS * hd), _f32)
        is_meta = lax.broadcasted_iota(jnp.int32, (c, 128), 0) >= c - N_META
        _, states = _run_interleaved(delta_step(
            mq, mk, mv, mba_ref[...], zeros, zeros, zeros, is_meta,
            [jnp.zeros((hd, hd), _f32) for _ in heads]))
        for h in heads:
            state_ref[h] = states[h]
        hq_ref[...] = mq[c - HIST_ROWS:]
        hk_ref[...] = mk[c - HIST_ROWS:]
        hv_ref[...] = mv[c - HIST_ROWS:]
        obuf_ref[...] = jnp.zeros_like(obuf_ref)

    q_raw, k_raw, v_raw = q_ref[...], k_ref[...], v_ref[...]
    slot = s % 2
    early, late = _merge_pieces(s <= 1, p_ref, pprev_ref, pmeta_ref, sa_ref, sb_ref,
                                lambda: obuf_ref[1 - slot],
                                x_ref, poolw_ref, pscale_ref, wpa_ref, wpb_ref, wout_ref, nw_ref,
                                out_ref, hn_ref)
    base, extra = divmod(len(early), 4)
    n_level_yields = 15 * (rows // c) * N_HEADS // LEVEL_GROUP
    outs, states = _run_interleaved(
        delta_step(q_raw, k_raw, v_raw, ba_ref[...], hq_ref[...], hk_ref[...], hv_ref[...], None,
                   [state_ref[h] for h in heads]),
        early + late,
        plan=[base + 1] * extra + [base] * (4 - extra) + _spread(len(late), n_level_yields))
    hq_ref[...] = q_raw[rows - HIST_ROWS:]
    hk_ref[...] = k_raw[rows - HIST_ROWS:]
    hv_ref[...] = v_raw[rows - HIST_ROWS:]
    for h in heads:
        state_ref[h] = states[h]
    blocks = []
    for ci in range(rows // c):
        gated = []
        for h in heads:
            o = outs[ci * N_HEADS + h]
            o = o * lax.rsqrt(jnp.mean(o * o, axis=-1, keepdims=True) + RMS_EPS)
            sl = slice(h * hd, (h + 1) * hd)
            z = zs_ref[ci * c:(ci + 1) * c, sl].astype(_f32)
            gated.append((o * onw_ref[...] * z).astype(obuf_ref.dtype))
        blocks.append(jnp.concatenate(gated, axis=1))
    obuf_ref[slot] = jnp.concatenate(blocks, axis=0)


def _mixer(lin, act, ba, meta_lin, meta_lin_pad, meta_ba_pad, conv_w, alog_row, dtb_row, onw_row, x,
           pool_w, pool_scale, w_proj_a, w_proj_b, w_out, ffn_norm_w, batch, p_part, zs_part):
    m, d = x.shape
    c = DELTA_CHUNK
    rows = MIXER_ROWS
    assert (m // batch) % rows == 0 and rows % c == 0
    nblk = m // batch // rows
    hd = HEAD_DIM
    w = N_HEADS * hd
    pd = pool_w.shape[0] * POOL_GROUP_DIM
    assert pd == w
    sub = rows // N_META

    def delta_row(b, s):
        return b * nblk + jnp.minimum(s, nblk - 1)

    def merge_row(b, s):
        return b * nblk + jnp.maximum(s - 1, 0)

    def tok(part):
        return pl.BlockSpec((rows, w), lambda b, s: (delta_row(b, s), part))

    def meta(part):
        return pl.BlockSpec((c, w), lambda b, s: (0, part))

    def cw(part):
        return pl.BlockSpec((CONV_K, w), lambda b, s: (0, part))

    def const(shape):
        return pl.BlockSpec(shape, lambda b, s: (0,) * len(shape), pipeline_mode=pl.Buffered(1))

    row128 = pl.BlockSpec((1, 128), lambda b, s: (0, 0))
    return pl.pallas_call(
        functools.partial(_mixer_kernel, nblk),
        grid=(batch, nblk + 1),
        in_specs=[
            tok(0), tok(1), tok(2), tok(zs_part),
            pl.BlockSpec((rows, 128), lambda b, s: (delta_row(b, s), 0)),
            meta(0), meta(1), meta(2),
            pl.BlockSpec((c, 128), lambda b, s: (0, 0)),
            cw(0), cw(1), cw(2),
            row128, row128, row128,
            pl.BlockSpec((rows, pd), lambda b, s: (merge_row(b, s), p_part)),
            pl.BlockSpec((N_META, pd), lambda b, s: (jnp.maximum(merge_row(b, s) * sub - 1, 0), p_part)),
            pl.BlockSpec((N_META, pd), lambda b, s: (0, p_part)),
            pl.BlockSpec((rows, d), lambda b, s: (merge_row(b, s), 0)),
            pl.BlockSpec((rows, d), lambda b, s: (merge_row(b, s), 1)),
            pl.BlockSpec((rows, d), lambda b, s: (merge_row(b, s), 0)),
            const(pool_w.shape), const((1, pd)), const(w_proj_a.shape), const(w_proj_b.shape),
            const(w_out.shape), const((1, d)),
        ],
        out_specs=[pl.BlockSpec((rows, d), lambda b, s: (merge_row(b, s), 0)),
                   pl.BlockSpec((rows, d), lambda b, s: (merge_row(b, s), 0))],
        out_shape=[jax.ShapeDtypeStruct((m, d), _f32), jax.ShapeDtypeStruct((m, d), _bf16)],
        scratch_shapes=[
            pltpu.VMEM((N_HEADS, hd, hd), _f32),
            pltpu.VMEM((HIST_ROWS, w), _f32),
            pltpu.VMEM((HIST_ROWS, w), _f32),
            pltpu.VMEM((HIST_ROWS, w), _f32),
            pltpu.VMEM((2, rows, w), _bf16),
        ],
        compiler_params=pltpu.CompilerParams(
            dimension_semantics=("arbitrary", "arbitrary"),
            vmem_limit_bytes=_V7X_VMEM_LIMIT),
        name="mixer",
    )(lin, lin, lin, act, ba, meta_lin_pad, meta_lin_pad, meta_lin_pad, meta_ba_pad,
      conv_w, conv_w, conv_w, alog_row, dtb_row, onw_row,
      lin, lin, meta_lin, act, act, x,
      pool_w, pool_scale, w_proj_a, w_proj_b, w_out, ffn_norm_w)


def _ffn_kernel(h_hbm, hn_ref, wg_ref, wu_ref, wd_ref, fw_ref, out_ref, h_buf, h_sem):
    i = pl.program_id(0)
    f = pl.program_id(1)
    tm = out_ref.shape[0]

    def residual_copy():
        rows = pl.ds(pl.multiple_of(i * tm, tm), tm)
        return pltpu.make_async_copy(h_hbm.at[rows], h_buf, h_sem)

    d = out_ref.shape[1]
    rc, dc = min(tm, 512), min(d, 512)

    def accumulate(r0, rows_per_chunk, first):
        rows = slice(r0, r0 + rows_per_chunk)
        hn = hn_ref[rows, :]
        act = (_silu(_dot(hn, wg_ref[...])) * _dot(hn, wu_ref[...])).astype(_bf16)
        for c0 in range(0, d, dc):
            part = _dot(act, wd_ref[:, c0:c0 + dc])
            if first:
                out_ref[rows, c0:c0 + dc] = part
            else:
                out_ref[rows, c0:c0 + dc] += part

    last = pl.num_programs(1) - 1

    @pl.when(f == 0)
    def _():
        residual_copy().start()
        for r0 in range(0, tm, rc):
            accumulate(r0, rc, True)

    @pl.when(jnp.logical_and(f > 0, f < last))
    def _():
        for r0 in range(0, tm, rc):
            accumulate(r0, rc, False)

    @pl.when(f == last)
    def _():
        residual_copy().wait()
        fc = min(tm, NORM_OVERLAP_ROWS)
        for r0 in range(0, tm, fc):
            accumulate(r0, fc, False)
            rows = slice(r0, r0 + fc)
            out_ref[rows, :] = _rms_norm(h_buf[rows, :] + out_ref[rows, :], fw_ref[...])


def _ffn(h, hn, w_gate, w_up, w_down, final_w, tm, tf):
    m, d = h.shape
    f = w_gate.shape[1]
    assert f // tf >= 2, "the first and the last ffn column step must be different steps"
    return pl.pallas_call(
        _ffn_kernel,
        grid=(m // tm, f // tf),
        in_specs=[
            pl.BlockSpec(memory_space=pl.ANY),
            pl.BlockSpec((tm, d), lambda i, j: (i, 0)),
            pl.BlockSpec((d, tf), lambda i, j: (0, j)),
            pl.BlockSpec((d, tf), lambda i, j: (0, j)),
            pl.BlockSpec((tf, d), lambda i, j: (j, 0)),
            pl.BlockSpec((1, d), lambda i, j: (0, 0)),
        ],
        out_specs=pl.BlockSpec((tm, d), lambda i, j: (i, 0)),
        out_shape=jax.ShapeDtypeStruct((m, d), _f32),
        scratch_shapes=[pltpu.VMEM((tm, d), _f32), pltpu.SemaphoreType.DMA(())],
        compiler_params=pltpu.CompilerParams(
            dimension_semantics=("arbitrary", "arbitrary"),
            vmem_limit_bytes=_V7X_VMEM_LIMIT),
        name="ffn",
    )(h, hn, w_gate, w_up, w_down, final_w)


def _largest_tile(n, cap):
    t = cap
    while n % t:
        t //= 2
    return t


def kernel(x, meta_tokens, norm_mix_w, w_in, conv_w, a_log, dt_bias, o_norm_w, w_proj_a, pool_w,
           pool_scale, w_proj_b, w_out, norm_ffn_w, w_ffn_gate, w_ffn_up, w_ffn_down, norm_final_w):
    assert w_in.shape[0] == 1, "single layer block"
    batch, seq, d = x.shape
    h, hd = N_HEADS, HEAD_DIM
    qk = h * hd
    pd = pool_w.shape[1] * POOL_GROUP_DIM
    assert seq % DELTA_CHUNK == 0 and N_META <= DELTA_CHUNK
    m = batch * seq

    w_t = w_in[0].T
    off_z = 3 * qk
    off_ba = 4 * qk
    off_p = off_ba + 2 * h
    off_g = off_p + pd
    tn = qk
    lin_cols, sig_cols = off_z + pd, 2 * d
    assert pd == qk and sig_cols % qk == 0
    row_offsets = ([0, qk, 2 * qk, off_p] + [off_g + c0 for c0 in range(0, sig_cols, tn)] + [off_z])
    w_main_t = _w_prep(w_t, row_offsets, tn, _largest_tile(d, 1024))
    w_ba_t = jnp.pad(w_t[off_ba:off_p], ((0, 128 - 2 * h), (0, 0))).astype(_bf16)
    p_part, zs_part = off_z // pd, sig_cols // qk

    x2 = x.reshape(m, d)
    nw = norm_mix_w[0].reshape(1, d)
    meta_lin, meta_ba = _in_proj(meta_tokens, nw, w_main_t, w_ba_t, N_META, tn, lin_cols, sig_cols,
                                 lin_only=True)
    lin, act, ba, wg_bf, wu_bf, wd_bf, wpa_bf, wpb_bf, wout_bf = _in_proj(
        x2, nw, w_main_t, w_ba_t, _largest_tile(m, 1024), tn, lin_cols, sig_cols,
        casts=(w_ffn_gate[0], w_ffn_up[0], w_ffn_down[0], w_proj_a[0], w_proj_b[0], w_out[0]))
    pad = ((DELTA_CHUNK - N_META, 0), (0, 0))
    meta_lin_pad = jnp.pad(meta_lin, pad)
    meta_ba_pad = jnp.pad(meta_ba, pad)

    lane_pad = lambda a: jnp.pad(a.reshape(1, h), ((0, 0), (h, 128 - 2 * h)))
    h1, hn2 = _mixer(lin, act, ba, meta_lin, meta_lin_pad, meta_ba_pad, conv_w[0], lane_pad(a_log[0]),
                     lane_pad(dt_bias[0]), o_norm_w[0].reshape(1, hd), x2, pool_w[0].astype(_bf16),
                     pool_scale[0].reshape(1, pd), wpa_bf, wpb_bf, wout_bf,
                     norm_ffn_w[0].reshape(1, d), batch, p_part, zs_part)

    out = _ffn(h1, hn2, wg_bf, wu_bf, wd_bf, norm_final_w.reshape(1, d), _largest_tile(m, 1024),
               _largest_tile(w_ffn_gate.shape[2], 512))
    return out.reshape(batch, seq, d)
```
